```python
import jax
import jax.numpy as jnp
from jax import lax
import numpy as np

D_MODEL = 1024
BATCH = 1
SEQ = 16384
DEPTH = 4

CHUNK = 64
N_META = 16
META_PAD = CHUNK - N_META
D_CONV = 512
CONV_WIDTH = 31
HG_HEADS = 4
HG_DK = 128
HG_DV = 128
D_HG_K = HG_HEADS * HG_DK
D_HG = HG_HEADS * HG_DV
F_FLOOR = 1e-30
ATT_Q_HEADS = 8
ATT_KV_HEADS = 2
ATT_HEAD_DIM = 64
ATT_GROUP = ATT_Q_HEADS // ATT_KV_HEADS
D_ATT = ATT_Q_HEADS * ATT_HEAD_DIM
D_KV = ATT_KV_HEADS * ATT_HEAD_DIM
WINDOW = 128
WINDOW_CHUNKS = WINDOW // CHUNK
N_BRANCH = 3
EPS = 1e-6
IN_SIZES = (2 * D_CONV, D_CONV, D_HG_K, D_HG_K, D_HG, D_HG, D_ATT, D_KV, D_KV, D_ATT, N_BRANCH * D_MODEL)
D_IN = 2 * D_CONV + D_CONV + 2 * D_HG_K + 2 * D_HG + 2 * D_ATT + 2 * D_KV + N_BRANCH * D_MODEL

kernel_name = "hybrid_conv_hgrn2_swa_meta_trunk"


def _rmsnorm(x, g):
    xf = x.astype(jnp.float32)
    return xf * lax.rsqrt(jnp.mean(xf * xf, axis=-1, keepdims=True) + EPS) * g.astype(jnp.float32)


def _conv_branch(a_in, a_gate, valid, conv_w, conv_b, ln_g, ln_b, w_o):
    a = a_in.astype(jnp.float32)
    u = a[..., :D_CONV] * jax.nn.sigmoid(a[..., D_CONV:])
    u = jnp.where(valid[None, :, None], u, 0.0)
    y = lax.conv_general_dilated(
        u, conv_w.astype(jnp.float32)[:, None, :], window_strides=(1,),
        padding=[(CONV_WIDTH - 1, 0)], dimension_numbers=("NWC", "WIO", "NWC"),
        feature_group_count=D_CONV)
    y = y + conv_b.astype(jnp.float32)
    mu = jnp.mean(y, axis=-1, keepdims=True)
    var = jnp.mean(jnp.square(y - mu), axis=-1, keepdims=True)
    y = (y - mu) * lax.rsqrt(var + EPS) * ln_g.astype(jnp.float32) + ln_b.astype(jnp.float32)
    y = jax.nn.silu(y) * jax.nn.silu(a_gate.astype(jnp.float32))
    return y.astype(w_o.dtype) @ w_o


def _hgrn2_branch(q, fz, i, gate, valid, lb, gn_g, w_o):
    B, L, _ = q.shape
    n_chunks = L // CHUNK
    qh = jax.nn.silu(q.astype(jnp.float32)).reshape(B, L, HG_HEADS, HG_DK)
    z = fz.astype(jnp.float32).reshape(B, L, HG_HEADS, HG_DK)
    lbh = lb.astype(jnp.float32).reshape(HG_HEADS, HG_DK)
    f = lbh + (1.0 - lbh) * jax.nn.sigmoid(z)
    logf = jnp.log(jnp.maximum(f, F_FLOOR))
    kh = (1.0 - lbh) * jax.nn.sigmoid(-z)
    vmask = valid[None, :, None, None]
    logf = jnp.where(vmask, logf, 0.0)
    kh = jnp.where(vmask, kh, 0.0)
    vh = i.astype(jnp.float32).reshape(B, L, HG_HEADS, HG_DV)

    def to_chunks(t):
        return t.reshape(B, n_chunks, CHUNK, HG_HEADS, -1).transpose(1, 0, 3, 2, 4)

    tri = jnp.tril(jnp.ones((CHUNK, CHUNK), dtype=bool))[:, :, None]

    def step(S, inp):
        qc, kc, vc, gc = inp
        b = jnp.cumsum(gc, axis=2)
        b_last = b[:, :, -1:, :]
        rel = b[:, :, :, None, :] - b[:, :, None, :, :]
        decay = jnp.exp(jnp.where(tri, rel, -jnp.inf))
        scores = jnp.einsum("bhtd,bhtsd,bhsd->bhts", qc, decay, kc)
        o = (jnp.einsum("bhts,bhsv->bhtv", scores, vc)
             + jnp.einsum("bhtd,bhdv->bhtv", qc * jnp.exp(b), S))
        S = (jnp.exp(b_last)[:, :, 0, :, None] * S
             + jnp.einsum("bhsd,bhsv->bhdv", kc * jnp.exp(b_last - b), vc))
        return S, o

    S0 = jnp.zeros((B, HG_HEADS, HG_DK, HG_DV), jnp.float32)
    _, o = lax.scan(step, S0, (to_chunks(qh), to_chunks(kh), to_chunks(vh), to_chunks(logf)))
    o = o.transpose(1, 0, 3, 2, 4).reshape(B, L, HG_HEADS, HG_DV)
    o = _rmsnorm(o, gn_g).reshape(B, L, D_HG) * jax.nn.silu(gate.astype(jnp.float32))
    return o.astype(w_o.dtype) @ w_o


def _swa_key_mask(n_chunks):
    c = jnp.arange(n_chunks)[:, None]
    j = jnp.arange(CHUNK)[None, :]
    meta_ok = (c > WINDOW_CHUNKS) & (j >= META_PAD)
    band = [((c - WINDOW_CHUNKS + r) * CHUNK + j) >= META_PAD for r in range(WINDOW_CHUNKS + 1)]
    return jnp.concatenate([meta_ok] + band, axis=1)


def _band(t, n_chunks):
    tp = jnp.pad(t, ((0, 0), (WINDOW_CHUNKS, 0), (0, 0), (0, 0), (0, 0)))
    meta = jnp.broadcast_to(t[:, :1], t.shape)
    return jnp.concatenate([meta] + [tp[:, r:r + n_chunks] for r in range(WINDOW_CHUNKS + 1)], axis=2)


def _swa_branch(q, k, v, gate, qn_g, kn_g, sinks, w_o, key_mask):
    B, L, _ = q.shape
    n_chunks = L // CHUNK
    qh = _rmsnorm(q.reshape(B, L, ATT_Q_HEADS, ATT_HEAD_DIM), qn_g).reshape(
        B, n_chunks, CHUNK, ATT_KV_HEADS, ATT_GROUP, ATT_HEAD_DIM)
    kh = _rmsnorm(k.reshape(B, L, ATT_KV_HEADS, ATT_HEAD_DIM), kn_g).reshape(
        B, n_chunks, CHUNK, ATT_KV_HEADS, ATT_HEAD_DIM)
    vh = v.astype(jnp.float32).reshape(B, n_chunks, CHUNK, ATT_KV_HEADS, ATT_HEAD_DIM)
    kb = _band(kh, n_chunks)
    vb = _band(vh, n_chunks)
    s = jnp.einsum("bnqhgd,bnkhd->bnhgqk", qh, kb) * (ATT_HEAD_DIM ** -0.5)
    s = jnp.where(key_mask[None, :, None, None, None, :], s, -jnp.inf)
    sink = sinks.astype(jnp.float32).reshape(ATT_KV_HEADS, ATT_GROUP)[None, None, :, :, None, None]
    m = jnp.maximum(jnp.max(s, axis=-1, keepdims=True), sink)
    p = jnp.exp(s - m)
    denom = jnp.sum(p, axis=-1, keepdims=True) + jnp.exp(sink - m)
    o = jnp.einsum("bnhgqk,bnkhd->bnqhgd", p / denom, vb).reshape(B, L, D_ATT)
    o = o * jax.nn.silu(gate.astype(jnp.float32))
    return o.astype(w_o.dtype) @ w_o


def setup_inputs(seed: int = 0) -> dict:
    key = jax.random.key(seed)
    ks = jax.random.split(key, 20)
    f32 = jnp.float32
    nrm = lambda k, shape, scale: jax.random.normal(k, shape, f32) * scale
    return {
        "x": nrm(ks[0], (BATCH, SEQ, D_MODEL), 1.0),
        "meta_tokens": nrm(ks[1], (N_META, D_MODEL), 1.0),
        "norm_g": 1.0 + nrm(ks[2], (DEPTH, D_MODEL), 0.05),
        "w_in": nrm(ks[3], (DEPTH, D_MODEL, D_IN), D_MODEL ** -0.5),
        "conv_w": nrm(ks[4], (DEPTH, CONV_WIDTH, D_CONV), CONV_WIDTH ** -0.5),
        "conv_b": nrm(ks[5], (DEPTH, D_CONV), 0.02),
        "conv_ln_g": 1.0 + nrm(ks[6], (DEPTH, D_CONV), 0.05),
        "conv_ln_b": nrm(ks[7], (DEPTH, D_CONV), 0.02),
        "w_conv_out": nrm(ks[8], (DEPTH, D_CONV, D_MODEL), D_CONV ** -0.5),
        "hg_lower_bounds": nrm(ks[9], (DEPTH, D_HG_K), 0.1),
        "hg_norm_g": 1.0 + nrm(ks[10], (DEPTH, HG_DV), 0.05),
        "w_hg_out": nrm(ks[11], (DEPTH, D_HG, D_MODEL), D_HG ** -0.5),
        "q_norm_g": 1.0 + nrm(ks[12], (DEPTH, ATT_HEAD_DIM), 0.05),
        "k_norm_g": 1.0 + nrm(ks[13], (DEPTH, ATT_HEAD_DIM), 0.05),
        "attn_sinks": nrm(ks[14], (DEPTH, ATT_Q_HEADS), 0.5),
        "w_att_out": nrm(ks[15], (DEPTH, D_ATT, D_MODEL), D_ATT ** -0.5),
        "w_out": nrm(ks[16], (DEPTH, D_MODEL, D_MODEL), D_MODEL ** -0.5),
    }


def reference(x, meta_tokens, norm_g, w_in, conv_w, conv_b, conv_ln_g, conv_ln_b, w_conv_out,
              hg_lower_bounds, hg_norm_g, w_hg_out, q_norm_g, k_norm_g, attn_sinks, w_att_out, w_out):
    B = x.shape[0]
    dt = x.dtype
    h = jnp.concatenate([
        jnp.zeros((B, META_PAD, D_MODEL), dt),
        jnp.broadcast_to(meta_tokens.astype(dt)[None], (B, N_META, D_MODEL)),
        x], axis=1)
    L = h.shape[1]
    n_chunks = L // CHUNK
    valid = jnp.arange(L) >= META_PAD
    key_mask = _swa_key_mask(n_chunks)
    lb_sm = jax.nn.softmax(hg_lower_bounds.astype(jnp.float32), axis=0)
    lb_all = jnp.clip(jnp.cumsum(lb_sm, axis=0) - lb_sm[0:1], 0.0, 1.0)
    split_idx = [int(v) for v in np.cumsum(IN_SIZES)[:-1]]

    for l in range(DEPTH):
        hn = _rmsnorm(h, norm_g[l]).astype(dt)
        u = hn @ w_in[l]
        (a_in, a_gate, b_q, b_f, b_i, b_gate, c_q, c_k, c_v, c_gate, g_logits) = jnp.split(u, split_idx, axis=-1)
        z_a = _conv_branch(a_in, a_gate, valid, conv_w[l], conv_b[l], conv_ln_g[l], conv_ln_b[l], w_conv_out[l])
        z_b = _hgrn2_branch(b_q, b_f, b_i, b_gate, valid, lb_all[l], hg_norm_g[l], w_hg_out[l])
        z_c = _swa_branch(c_q, c_k, c_v, c_gate, q_norm_g[l], k_norm_g[l], attn_sinks[l], w_att_out[l], key_mask)
        g = jax.nn.sigmoid(g_logits.astype(jnp.float32))
        mixed = (g[..., :D_MODEL] * z_a.astype(jnp.float32)
                 + g[..., D_MODEL:2 * D_MODEL] * z_b.astype(jnp.float32)
                 + g[..., 2 * D_MODEL:] * z_c.astype(jnp.float32))
        h = h + mixed.astype(dt) @ w_out[l]

    return h[:, CHUNK:]
```

```python
import jax
import jax.numpy as jnp
from jax import lax
from jax.experimental import pallas as pl
from jax.experimental.pallas import tpu as pltpu

F32 = jnp.float32
BF16 = jnp.bfloat16

D_MODEL = 1024
SEQ = 16384
DEPTH = 4
CHUNK = 64
N_META = 16
D_CONV = 512
CONV_WIDTH = 31
HG_HEADS = 4
HG_D = 128
D_HG = HG_HEADS * HG_D
F_FLOOR = 1e-30
ATT_Q_HEADS = 8
ATT_KV_HEADS = 2
HEAD_DIM = 64
ATT_GROUP = ATT_Q_HEADS // ATT_KV_HEADS
D_ATT = ATT_Q_HEADS * HEAD_DIM
D_KV = ATT_KV_HEADS * HEAD_DIM
WINDOW_CHUNKS = 2
EPS = 1e-6

N_A = 2 * D_CONV + D_CONV
N_HG = 4 * D_HG
N_SW = 2 * D_ATT + 2 * D_KV
N_G = 3 * D_MODEL
OFF_HG = N_A
OFF_SW = OFF_HG + N_HG
OFF_G = OFF_SW + N_SW

TILE = 256
NC = TILE // CHUNK
N_TILES = SEQ // TILE
FIRST_VALID = TILE - N_META
CONV_PAD = 32
CONV_RB = 32
SUB = 16
BAND = (WINDOW_CHUNKS + 1) * CHUNK
KV_HEAD = CHUNK + WINDOW_CHUNKS * CHUNK
VMEM_LIMIT_BYTES = 56 * 1024 * 1024

NT_DIMS = (((1,), (1,)), ((), ()))
TN_DIMS = (((0,), (0,)), ((), ()))


def _dot(a, b):
    return jnp.dot(a, b, preferred_element_type=F32)


def _silu(x):
    return x * jax.nn.sigmoid(x)


def _hgrn_head(q, k, v, b, st_scr, hd):
    b_last = b[CHUNK - 1:CHUNK, :]
    st = st_scr[hd]
    q_dec = (q * jnp.exp(b)).astype(BF16)
    o = lax.dot_general(q_dec, st.astype(BF16), NT_DIMS, preferred_element_type=F32)
    k_dec = (k * jnp.exp(b_last - b)).astype(BF16)
    v16 = v.astype(BF16)
    upd = lax.dot_general(v16, k_dec, TN_DIMS, preferred_element_type=F32)
    st_scr[hd] = st * jnp.exp(b_last) + upd

    row = lax.broadcasted_iota(jnp.int32, (CHUNK, 1), 0)
    lane = lax.broadcasted_iota(jnp.int32, (SUB, CHUNK), 1)
    sub_row = lax.broadcasted_iota(jnp.int32, (SUB, CHUNK), 0)
    blocks = []
    for i in range(CHUNK // SUB):
        lo = i * SUB
        qi = q[lo:lo + SUB]
        bi = b[lo:lo + SUB]
        ki = k[lo:lo + SUB]
        if i == 0:
            sc = jnp.zeros((SUB, CHUNK), F32)
        else:
            b_ref = b[lo - 1:lo, :]
            qt = (qi * jnp.exp(bi - b_ref)).astype(BF16)
            kt = jnp.where(row < lo, k * jnp.exp(jnp.minimum(b_ref - b, 0.0)), 0.0).astype(BF16)
            sc = lax.dot_general(qt, kt, NT_DIMS, preferred_element_type=F32)
        for s in range(SUB):
            x = qi * ki[s:s + 1, :] * jnp.exp(jnp.minimum(bi - bi[s:s + 1, :], 0.0))
            col = jnp.sum(x, axis=-1, keepdims=True)
            sc = jnp.where((lane == lo + s) & (sub_row >= s), col, sc)
        blocks.append(sc)
    scores = jnp.concatenate(blocks, axis=0).astype(BF16)
    return o + _dot(scores, v16)


def _layer_kernel(layer_ref, hm_ref, hx_ref, ng_ref, wa_ref, whg_ref, wsw_ref, wg_ref,
                  cw_ref, cb_ref, lng_ref, lnb_ref, wco_ref, lb_ref, hgn_ref, who_ref,
                  qn_ref, kn_ref, sinks_ref, wao_ref, wout_ref,
                  om_ref, ox_ref,
                  hn_scr, mix_scr, ext_scr, ag_scr, za_scr, uh_scr, zb_scr, st_scr,
                  q_scr, kext_scr, vext_scr, cg_scr, zc_scr,
                  tri_scr, bdq_scr, bdk_scr, lbrow_scr):
    i = pl.program_id(0)
    layer = layer_ref[0]
    first_valid = jnp.where(i > 0, 0, FIRST_VALID)

    @pl.when(i == 0)
    def _init():
        ext_scr[0:CONV_PAD, :] = jnp.zeros((CONV_PAD, D_CONV), F32)
        st_scr[...] = jnp.zeros(st_scr.shape, F32)
        kext_scr[0:KV_HEAD, :] = jnp.zeros((KV_HEAD, D_KV), BF16)
        vext_scr[0:KV_HEAD, :] = jnp.zeros((KV_HEAD, D_KV), BF16)
        r = lax.broadcasted_iota(jnp.int32, (CHUNK, CHUNK), 0)
        c = lax.broadcasted_iota(jnp.int32, (CHUNK, CHUNK), 1)
        tri_scr[...] = jnp.where(c <= r, 1.0, 0.0).astype(BF16)
        r = lax.broadcasted_iota(jnp.int32, (D_ATT, D_ATT), 0) // HEAD_DIM
        c = lax.broadcasted_iota(jnp.int32, (D_ATT, D_ATT), 1) // HEAD_DIM
        bdq_scr[...] = jnp.where(r == c, 1.0, 0.0).astype(BF16)
        r = lax.broadcasted_iota(jnp.int32, (D_KV, D_KV), 0) // HEAD_DIM
        c = lax.broadcasted_iota(jnp.int32, (D_KV, D_KV), 1) // HEAD_DIM
        bdk_scr[...] = jnp.where(r == c, 1.0, 0.0).astype(BF16)
        rows = [lb_ref[j:j + 1, :] for j in range(DEPTH)]
        mx = rows[0]
        for j in range(1, DEPTH):
            mx = jnp.maximum(mx, rows[j])
        ex = [jnp.exp(rw - mx) for rw in rows]
        tot = ex[0]
        for j in range(1, DEPTH):
            tot = tot + ex[j]
        sm = [e / tot for e in ex]
        cum = sm[0]
        lb_sel = jnp.zeros((1, D_HG), F32)
        for j in range(DEPTH):
            if j > 0:
                cum = cum + sm[j]
            lb_j = jnp.clip(cum - sm[0], 0.0, 1.0)
            lb_sel = jnp.where(layer == j, lb_j, lb_sel)
        lbrow_scr[...] = lb_sel

    h = jnp.where(i == 0, hm_ref[...], hx_ref[...])
    ms = jnp.mean(h * h, axis=-1, keepdims=True)
    hn_scr[...] = (h * lax.rsqrt(ms + EPS) * ng_ref[...]).astype(BF16)

    a = _dot(hn_scr[...], wa_ref[:, 0:2 * D_CONV])
    u = a[:, 0:D_CONV] * jax.nn.sigmoid(a[:, D_CONV:2 * D_CONV])
    row_t = lax.broadcasted_iota(jnp.int32, (TILE, 1), 0)
    ext_scr[CONV_PAD:CONV_PAD + TILE, :] = jnp.where(row_t >= first_valid, u, 0.0)
    ag_scr[...] = _dot(hn_scr[...], wa_ref[:, 2 * D_CONV:N_A])
    for rb in range(TILE // CONV_RB):
        r0 = rb * CONV_RB
        acc = jnp.broadcast_to(cb_ref[...], (CONV_RB, D_CONV))
        for j in range(CONV_WIDTH):
            start = r0 + CONV_PAD - (CONV_WIDTH - 1) + j
            acc = acc + cw_ref[j:j + 1, :] * ext_scr[start:start + CONV_RB, :]
        mu = jnp.mean(acc, axis=-1, keepdims=True)
        d = acc - mu
        var = jnp.mean(d * d, axis=-1, keepdims=True)
        y = d * lax.rsqrt(var + EPS) * lng_ref[...] + lnb_ref[...]
        za_scr[r0:r0 + CONV_RB, :] = (_silu(y) * _silu(ag_scr[r0:r0 + CONV_RB, :])).astype(BF16)
    ext_scr[0:CONV_PAD, :] = ext_scr[TILE:TILE + CONV_PAD, :]
    z = _dot(za_scr[...], wco_ref[...])
    g = jax.nn.sigmoid(_dot(hn_scr[...], wg_ref[:, 0:D_MODEL]))
    mix_scr[...] = g * z

    uh_scr[...] = _dot(hn_scr[...], whg_ref[...])

    def hg_chunk(c, carry):
        r0 = pl.multiple_of(c * CHUNK, CHUNK)
        rows = pl.ds(r0, CHUNK)
        vrow = (lax.broadcasted_iota(jnp.int32, (CHUNK, 1), 0) + r0) >= first_valid
        zf = uh_scr[rows, D_HG:2 * D_HG]
        lb = lbrow_scr[...]
        f = lb + (1.0 - lb) * jax.nn.sigmoid(zf)
        logf = jnp.where(vrow, jnp.log(jnp.maximum(f, F_FLOOR)), 0.0)
        kk = jnp.where(vrow, (1.0 - lb) * jax.nn.sigmoid(-zf), 0.0)
        p0 = logf.astype(BF16)
        r1 = logf - p0.astype(F32)
        p1 = r1.astype(BF16)
        p2 = (r1 - p1.astype(F32)).astype(BF16)
        tri = tri_scr[...]
        b = _dot(tri, p0) + _dot(tri, p1) + _dot(tri, p2)
        qs = _silu(uh_scr[rows, 0:D_HG])
        vv = uh_scr[rows, 2 * D_HG:3 * D_HG]
        outs = []
        for hd in range(HG_HEADS):
            sl = slice(hd * HG_D, (hd + 1) * HG_D)
            o = _hgrn_head(qs[:, sl], kk[:, sl], vv[:, sl], b[:, sl], st_scr, hd)
            o = o * lax.rsqrt(jnp.mean(o * o, axis=-1, keepdims=True) + EPS) * hgn_ref[...]
            outs.append(o)
        o_all = jnp.concatenate(outs, axis=-1)
        zb_scr[rows, :] = (o_all * _silu(uh_scr[rows, 3 * D_HG:4 * D_HG])).astype(BF16)
        return carry

    lax.fori_loop(0, NC, hg_chunk, 0)
    z = _dot(zb_scr[...], who_ref[...])
    g = jax.nn.sigmoid(_dot(hn_scr[...], wg_ref[:, D_MODEL:2 * D_MODEL]))
    mix_scr[...] += g * z

    usw = _dot(hn_scr[...], wsw_ref[...])
    q = usw[:, 0:D_ATT]
    ssq = _dot((q * q).astype(BF16), bdq_scr[...])
    q_scr[...] = (q * lax.rsqrt(ssq * (1.0 / HEAD_DIM) + EPS)
                  * (qn_ref[...] * (HEAD_DIM ** -0.5))).astype(BF16)
    k = usw[:, D_ATT:D_ATT + D_KV]
    ssk = _dot((k * k).astype(BF16), bdk_scr[...])
    kext_scr[KV_HEAD:KV_HEAD + TILE, :] = (
        k * lax.rsqrt(ssk * (1.0 / HEAD_DIM) + EPS) * kn_ref[...]).astype(BF16)
    vext_scr[KV_HEAD:KV_HEAD + TILE, :] = usw[:, D_ATT + D_KV:D_ATT + 2 * D_KV].astype(BF16)
    cg_scr[...] = usw[:, D_ATT + 2 * D_KV:N_SW]

    def sw_chunk(c, carry):
        r0 = pl.multiple_of(c * CHUNK, CHUNK)
        rows = pl.ds(r0, CHUNK)
        cg = i * NC + c
        band = pl.ds(r0 + CHUNK, BAND)
        kb = jnp.concatenate([kext_scr[band, :], kext_scr[0:CHUNK, :]], axis=0)
        vb = jnp.concatenate([vext_scr[band, :], vext_scr[0:CHUNK, :]], axis=0)
        col = lax.broadcasted_iota(jnp.int32, (1, BAND + CHUNK), 1)
        band_lo = FIRST_VALID - (cg - WINDOW_CHUNKS) * CHUNK
        meta_lo = jnp.where(cg > NC - 1 + WINDOW_CHUNKS, BAND + CHUNK - N_META, BAND + CHUNK)
        ok = ((col >= band_lo) & (col < BAND)) | (col >= meta_lo)
        bias = jnp.where(ok, 0.0, -jnp.inf)
        outs = []
        for hq in range(ATT_Q_HEADS):
            kv = hq // ATT_GROUP
            qh = q_scr[rows, hq * HEAD_DIM:(hq + 1) * HEAD_DIM]
            s = lax.dot_general(qh, kb[:, kv * HEAD_DIM:(kv + 1) * HEAD_DIM], NT_DIMS,
                                preferred_element_type=F32) + bias
            sink = sinks_ref[layer, hq]
            m = jnp.maximum(jnp.max(s, axis=-1, keepdims=True), sink)
            p = jnp.exp(s - m)
            den = jnp.sum(p, axis=-1, keepdims=True) + jnp.exp(sink - m)
            outs.append(_dot(p.astype(BF16), vb[:, kv * HEAD_DIM:(kv + 1) * HEAD_DIM]) / den)
        o_all = jnp.concatenate(outs, axis=-1)
        zc_scr[rows, :] = (o_all * _silu(cg_scr[rows, :])).astype(BF16)
        return carry

    lax.fori_loop(0, NC, sw_chunk, 0)

    @pl.when(i == 0)
    def _keep_meta():
        kext_scr[0:CHUNK, :] = kext_scr[KV_HEAD + TILE - CHUNK:KV_HEAD + TILE, :]
        vext_scr[0:CHUNK, :] = vext_scr[KV_HEAD + TILE - CHUNK:KV_HEAD + TILE, :]

    kext_scr[CHUNK:KV_HEAD, :] = kext_scr[TILE + CHUNK:TILE + KV_HEAD, :]
    vext_scr[CHUNK:KV_HEAD, :] = vext_scr[TILE + CHUNK:TILE + KV_HEAD, :]
    z = _dot(zc_scr[...], wao_ref[...])
    g = jax.nn.sigmoid(_dot(hn_scr[...], wg_ref[:, 2 * D_MODEL:3 * D_MODEL]))
    mix_scr[...] += g * z

    delta = _dot(mix_scr[...].astype(BF16), wout_ref[...])

    @pl.when(i == 0)
    def _store_meta():
        om_ref[...] = hm_ref[...] + delta
        ox_ref[...] = hx_ref[...]

    @pl.when(i > 0)
    def _store():
        ox_ref[...] = hx_ref[...] + delta


def _layer_call():
    def per_layer(shape):
        return pl.BlockSpec((None,) + shape, lambda i, l: (l[0],) + (0,) * len(shape),
                            pipeline_mode=pl.Buffered(1))

    def whole(shape):
        return pl.BlockSpec(shape, lambda i, l: (0,) * len(shape), pipeline_mode=pl.Buffered(1))

    x_spec = pl.BlockSpec((TILE, D_MODEL), lambda i, l: (jnp.maximum(i - 1, 0), 0))
    m_spec = pl.BlockSpec((TILE, D_MODEL), lambda i, l: (0, 0))
    in_specs = [
        m_spec,
        x_spec,
        per_layer((1, D_MODEL)),
        per_layer((D_MODEL, N_A)),
        per_layer((D_MODEL, N_HG)),
        per_layer((D_MODEL, N_SW)),
        per_layer((D_MODEL, N_G)),
        per_layer((CONV_WIDTH, D_CONV)),
        per_layer((1, D_CONV)),
        per_layer((1, D_CONV)),
        per_layer((1, D_CONV)),
        per_layer((D_CONV, D_MODEL)),
        whole((DEPTH, D_HG)),
        per_layer((1, HG_D)),
        per_layer((D_HG, D_MODEL)),
        per_layer((1, D_ATT)),
        per_layer((1, D_KV)),
        pl.BlockSpec(memory_space=pltpu.SMEM),
        per_layer((D_ATT, D_MODEL)),
        per_layer((D_MODEL, D_MODEL)),
    ]
    scratch = [
        pltpu.VMEM((TILE, D_MODEL), BF16),
        pltpu.VMEM((TILE, D_MODEL), F32),
        pltpu.VMEM((TILE + CONV_PAD, D_CONV), F32),
        pltpu.VMEM((TILE, D_CONV), F32),
        pltpu.VMEM((TILE, D_CONV), BF16),
        pltpu.VMEM((TILE, N_HG), F32),
        pltpu.VMEM((TILE, D_HG), BF16),
        pltpu.VMEM((HG_HEADS, HG_D, HG_D), F32),
        pltpu.VMEM((TILE, D_ATT), BF16),
        pltpu.VMEM((KV_HEAD + TILE, D_KV), BF16),
        pltpu.VMEM((KV_HEAD + TILE, D_KV), BF16),
        pltpu.VMEM((TILE, D_ATT), F32),
        pltpu.VMEM((TILE, D_ATT), BF16),
        pltpu.VMEM((CHUNK, CHUNK), BF16),
        pltpu.VMEM((D_ATT, D_ATT), BF16),
        pltpu.VMEM((D_KV, D_KV), BF16),
        pltpu.VMEM((1, D_HG), F32),
    ]
    return pl.pallas_call(
        _layer_kernel,
        grid_spec=pltpu.PrefetchScalarGridSpec(
            num_scalar_prefetch=1,
            grid=(N_TILES + 1,),
            in_specs=in_specs,
            out_specs=[m_spec, x_spec],
            scratch_shapes=scratch,
        ),
        out_shape=[jax.ShapeDtypeStruct((TILE, D_MODEL), F32),
                   jax.ShapeDtypeStruct((SEQ, D_MODEL), F32)],
        input_output_aliases={2: 1},
        compiler_params=pltpu.CompilerParams(
            dimension_semantics=("arbitrary",),
            vmem_limit_bytes=VMEM_LIMIT_BYTES,
        ),
        name="trunk_layer",
    )


def kernel(x, meta_tokens, norm_g, w_in, conv_w, conv_b, conv_ln_g, conv_ln_b, w_conv_out,
           hg_lower_bounds, hg_norm_g, w_hg_out, q_norm_g, k_norm_g, attn_sinks, w_att_out, w_out):
    assert x.shape == (1, SEQ, D_MODEL) and w_in.shape == (DEPTH, D_MODEL, OFF_G + N_G)
    hx = x[0].astype(F32)
    hm = jnp.concatenate([jnp.zeros((FIRST_VALID, D_MODEL), F32), meta_tokens.astype(F32)], axis=0)
    w16 = w_in.astype(BF16)
    params = (
        norm_g.astype(F32)[:, None, :],
        w16[:, :, 0:N_A], w16[:, :, OFF_HG:OFF_SW], w16[:, :, OFF_SW:OFF_G], w16[:, :, OFF_G:],
        conv_w.astype(F32), conv_b.astype(F32)[:, None, :],
        conv_ln_g.astype(F32)[:, None, :], conv_ln_b.astype(F32)[:, None, :],
        w_conv_out.astype(BF16),
        hg_lower_bounds.astype(F32), hg_norm_g.astype(F32)[:, None, :], w_hg_out.astype(BF16),
        jnp.tile(q_norm_g.astype(F32), (1, ATT_Q_HEADS))[:, None, :],
        jnp.tile(k_norm_g.astype(F32), (1, ATT_KV_HEADS))[:, None, :],
        attn_sinks.astype(F32), w_att_out.astype(BF16), w_out.astype(BF16),
    )
    call = _layer_call()

    def body(l, carry):
        om, ox = call(jnp.reshape(l, (1,)).astype(jnp.int32), carry[0], carry[1], *params)
        return om, ox

    _, hx = lax.fori_loop(0, DEPTH, body, (hm, hx))
    return hx[None].astype(x.dtype)
```

```python
import jax
import jax.numpy as jnp
from jax import lax
from jax.experimental import pallas as pl
from jax.experimental.pallas import tpu as pltpu

F32 = jnp.float32
BF16 = jnp.bfloat16

D_MODEL = 1024
SEQ = 16384
DEPTH = 4
CHUNK = 64
N_META = 16
D_CONV = 512
CONV_WIDTH = 31
HG_HEADS = 4
HG_D = 128
D_HG = HG_HEADS * HG_D
F_FLOOR = 1e-30
ATT_Q_HEADS = 8
ATT_KV_HEADS = 2
HEAD_DIM = 64
ATT_GROUP = ATT_Q_HEADS // ATT_KV_HEADS
D_ATT = ATT_Q_HEADS * HEAD_DIM
D_KV = ATT_KV_HEADS * HEAD_DIM
WINDOW_CHUNKS = 2
EPS = 1e-6

N_A = 2 * D_CONV + D_CONV
N_HG = 4 * D_HG
N_SW = 2 * D_ATT + 2 * D_KV
N_G = 3 * D_MODEL
OFF_HG = N_A
OFF_SW = OFF_HG + N_HG
OFF_G = OFF_SW + N_SW

TILE = 256
NC = TILE // CHUNK
N_TILES = SEQ // TILE
FIRST_VALID = TILE - N_META
CONV_PAD = 32
CONV_RB = 32
SUBLANES = 8
CONV_SH_ROWS = TILE + CONV_PAD - SUBLANES
SUB = 16
BAND = (WINDOW_CHUNKS + 1) * CHUNK
KV_HEAD = CHUNK + WINDOW_CHUNKS * CHUNK
VMEM_LIMIT_BYTES = 56 * 1024 * 1024

NT_DIMS = (((1,), (1,)), ((), ()))
TN_DIMS = (((0,), (0,)), ((), ()))


def _dot(a, b):
    return jnp.dot(a, b, preferred_element_type=F32)


def _silu(x):
    return x * jax.nn.sigmoid(x)


def _hgrn_head(q, k, v, b, st_scr, hd):
    b_last = b[CHUNK - 1:CHUNK, :]
    st = st_scr[hd]
    q_dec = (q * jnp.exp(b)).astype(BF16)
    o = lax.dot_general(q_dec, st.astype(BF16), NT_DIMS, preferred_element_type=F32)
    k_dec = (k * jnp.exp(b_last - b)).astype(BF16)
    v16 = v.astype(BF16)
    upd = lax.dot_general(v16, k_dec, TN_DIMS, preferred_element_type=F32)
    st_scr[hd] = st * jnp.exp(b_last) + upd

    row = lax.broadcasted_iota(jnp.int32, (CHUNK, 1), 0)
    lane = lax.broadcasted_iota(jnp.int32, (SUB, CHUNK), 1)
    sub_row = lax.broadcasted_iota(jnp.int32, (SUB, CHUNK), 0)
    blocks = []
    for i in range(CHUNK // SUB):
        lo = i * SUB
        qi = q[lo:lo + SUB]
        bi = b[lo:lo + SUB]
        ki = k[lo:lo + SUB]
        if i == 0:
            sc = jnp.zeros((SUB, CHUNK), F32)
        else:
            b_ref = b[lo - 1:lo, :]
            qt = (qi * jnp.exp(bi - b_ref)).astype(BF16)
            kt = jnp.where(row < lo, k * jnp.exp(jnp.minimum(b_ref - b, 0.0)), 0.0).astype(BF16)
            sc = lax.dot_general(qt, kt, NT_DIMS, preferred_element_type=F32)
        for s in range(SUB):
            x = qi * ki[s:s + 1, :] * jnp.exp(jnp.minimum(bi - bi[s:s + 1, :], 0.0))
            col = jnp.sum(x, axis=-1, keepdims=True)
            sc = jnp.where((lane == lo + s) & (sub_row >= s), col, sc)
        blocks.append(sc)
    scores = jnp.concatenate(blocks, axis=0).astype(BF16)
    return o + _dot(scores, v16)


def _layer_kernel(layer_ref, hm_ref, hx_ref, ng_ref, wa_ref, whg_ref, wsw_ref, wg_ref,
                  cw_ref, cb_ref, lng_ref, lnb_ref, wco_ref, lb_ref, hgn_ref, who_ref,
                  qn_ref, kn_ref, sinks_ref, wao_ref, wout_ref,
                  om_ref, ox_ref,
                  hn_scr, mix_scr, ext_scr, sh_scr, ag_scr, za_scr, uh_scr, zb_scr, st_scr,
                  q_scr, kext_scr, vext_scr, oh_scr, cg_scr, zc_scr,
                  tri_scr, bdq_scr, bdk_scr, lbrow_scr):
    i = pl.program_id(0)
    layer = layer_ref[0]
    first_valid = jnp.where(i > 0, 0, FIRST_VALID)

    @pl.when(i == 0)
    def _init():
        ext_scr[0:CONV_PAD, :] = jnp.zeros((CONV_PAD, D_CONV), F32)
        st_scr[...] = jnp.zeros(st_scr.shape, F32)
        kext_scr[:, 0:KV_HEAD, :] = jnp.zeros((ATT_KV_HEADS, KV_HEAD, HEAD_DIM), BF16)
        vext_scr[:, 0:KV_HEAD, :] = jnp.zeros((ATT_KV_HEADS, KV_HEAD, HEAD_DIM), BF16)
        r = lax.broadcasted_iota(jnp.int32, (CHUNK, CHUNK), 0)
        c = lax.broadcasted_iota(jnp.int32, (CHUNK, CHUNK), 1)
        tri_scr[...] = jnp.where(c <= r, 1.0, 0.0).astype(BF16)
        r = lax.broadcasted_iota(jnp.int32, (D_ATT, D_ATT), 0) // HEAD_DIM
        c = lax.broadcasted_iota(jnp.int32, (D_ATT, D_ATT), 1) // HEAD_DIM
        bdq_scr[...] = jnp.where(r == c, 1.0, 0.0).astype(BF16)
        r = lax.broadcasted_iota(jnp.int32, (D_KV, D_KV), 0) // HEAD_DIM
        c = lax.broadcasted_iota(jnp.int32, (D_KV, D_KV), 1) // HEAD_DIM
        bdk_scr[...] = jnp.where(r == c, 1.0, 0.0).astype(BF16)
        rows = [lb_ref[j:j + 1, :] for j in range(DEPTH)]
        mx = rows[0]
        for j in range(1, DEPTH):
            mx = jnp.maximum(mx, rows[j])
        ex = [jnp.exp(rw - mx) for rw in rows]
        tot = ex[0]
        for j in range(1, DEPTH):
            tot = tot + ex[j]
        sm = [e / tot for e in ex]
        cum = sm[0]
        lb_sel = jnp.zeros((1, D_HG), F32)
        for j in range(DEPTH):
            if j > 0:
                cum = cum + sm[j]
            lb_j = jnp.clip(cum - sm[0], 0.0, 1.0)
            lb_sel = jnp.where(layer == j, lb_j, lb_sel)
        lbrow_scr[...] = lb_sel

    h = jnp.where(i == 0, hm_ref[...], hx_ref[...])
    ms = jnp.mean(h * h, axis=-1, keepdims=True)
    hn_scr[...] = (h * lax.rsqrt(ms + EPS) * ng_ref[...]).astype(BF16)

    a = _dot(hn_scr[...], wa_ref[:, 0:2 * D_CONV])
    u = a[:, 0:D_CONV] * jax.nn.sigmoid(a[:, D_CONV:2 * D_CONV])
    row_t = lax.broadcasted_iota(jnp.int32, (TILE, 1), 0)
    ext_scr[CONV_PAD:CONV_PAD + TILE, :] = jnp.where(row_t >= first_valid, u, 0.0)
    ag_scr[...] = _dot(hn_scr[...], wa_ref[:, 2 * D_CONV:N_A])
    for r in range(1, SUBLANES):
        sh_scr[r - 1] = ext_scr[r:r + CONV_SH_ROWS, :]
    for rb in range(TILE // CONV_RB):
        r0 = rb * CONV_RB
        acc = jnp.broadcast_to(cb_ref[...], (CONV_RB, D_CONV))
        for j in range(CONV_WIDTH):
            m = CONV_PAD - (CONV_WIDTH - 1) + j
            a0 = r0 + m - m % SUBLANES
            if m % SUBLANES == 0:
                src = ext_scr[a0:a0 + CONV_RB, :]
            else:
                src = sh_scr[m % SUBLANES - 1, a0:a0 + CONV_RB, :]
            acc = acc + cw_ref[j:j + 1, :] * src
        mu = jnp.mean(acc, axis=-1, keepdims=True)
        d = acc - mu
        var = jnp.mean(d * d, axis=-1, keepdims=True)
        y = d * lax.rsqrt(var + EPS) * lng_ref[...] + lnb_ref[...]
        za_scr[r0:r0 + CONV_RB, :] = (_silu(y) * _silu(ag_scr[r0:r0 + CONV_RB, :])).astype(BF16)
    ext_scr[0:CONV_PAD, :] = ext_scr[TILE:TILE + CONV_PAD, :]
    z = _dot(za_scr[...], wco_ref[...])
    g = jax.nn.sigmoid(_dot(hn_scr[...], wg_ref[:, 0:D_MODEL]))
    mix_scr[...] = g * z

    uh_scr[...] = _dot(hn_scr[...], whg_ref[...])

    for c in range(NC):
        r0 = c * CHUNK
        rows = slice(r0, r0 + CHUNK)
        vrow = (lax.broadcasted_iota(jnp.int32, (CHUNK, 1), 0) + r0) >= first_valid
        zf = uh_scr[rows, D_HG:2 * D_HG]
        lb = lbrow_scr[...]
        f = lb + (1.0 - lb) * jax.nn.sigmoid(zf)
        logf = jnp.where(vrow, jnp.log(jnp.maximum(f, F_FLOOR)), 0.0)
        kk = jnp.where(vrow, (1.0 - lb) * jax.nn.sigmoid(-zf), 0.0)
        p0 = logf.astype(BF16)
        r1 = logf - p0.astype(F32)
        p1 = r1.astype(BF16)
        p2 = (r1 - p1.astype(F32)).astype(BF16)
        tri = tri_scr[...]
        b = _dot(tri, p0) + _dot(tri, p1) + _dot(tri, p2)
        qs = _silu(uh_scr[rows, 0:D_HG])
        vv = uh_scr[rows, 2 * D_HG:3 * D_HG]
        outs = []
        for hd in range(HG_HEADS):
            sl = slice(hd * HG_D, (hd + 1) * HG_D)
            o = _hgrn_head(qs[:, sl], kk[:, sl], vv[:, sl], b[:, sl], st_scr, hd)
            o = o * lax.rsqrt(jnp.mean(o * o, axis=-1, keepdims=True) + EPS) * hgn_ref[...]
            outs.append(o)
        o_all = jnp.concatenate(outs, axis=-1)
        zb_scr[rows, :] = (o_all * _silu(uh_scr[rows, 3 * D_HG:4 * D_HG])).astype(BF16)
    z = _dot(zb_scr[...], who_ref[...])
    g = jax.nn.sigmoid(_dot(hn_scr[...], wg_ref[:, D_MODEL:2 * D_MODEL]))
    mix_scr[...] += g * z

    usw = _dot(hn_scr[...], wsw_ref[...])
    q = usw[:, 0:D_ATT]
    ssq = _dot((q * q).astype(BF16), bdq_scr[...])
    qn = (q * lax.rsqrt(ssq * (1.0 / HEAD_DIM) + EPS)
          * (qn_ref[...] * (HEAD_DIM ** -0.5))).astype(BF16)
    for hq in range(ATT_Q_HEADS):
        q_scr[hq] = qn[:, hq * HEAD_DIM:(hq + 1) * HEAD_DIM]
    k = usw[:, D_ATT:D_ATT + D_KV]
    ssk = _dot((k * k).astype(BF16), bdk_scr[...])
    kn = (k * lax.rsqrt(ssk * (1.0 / HEAD_DIM) + EPS) * kn_ref[...]).astype(BF16)
    vv = usw[:, D_ATT + D_KV:D_ATT + 2 * D_KV].astype(BF16)
    for kv in range(ATT_KV_HEADS):
        kext_scr[kv, KV_HEAD:KV_HEAD + TILE, :] = kn[:, kv * HEAD_DIM:(kv + 1) * HEAD_DIM]
        vext_scr[kv, KV_HEAD:KV_HEAD + TILE, :] = vv[:, kv * HEAD_DIM:(kv + 1) * HEAD_DIM]
    cg_scr[...] = usw[:, D_ATT + 2 * D_KV:N_SW]

    grow = lax.broadcasted_iota(jnp.int32, (ATT_GROUP * CHUNK, 1), 0)
    sink_cols = []
    for kv in range(ATT_KV_HEADS):
        sc = jnp.zeros((ATT_GROUP * CHUNK, 1), F32)
        for j in range(ATT_GROUP):
            sc = jnp.where(grow // CHUNK == j, sinks_ref[layer, kv * ATT_GROUP + j], sc)
        sink_cols.append(sc)
    col = lax.broadcasted_iota(jnp.int32, (1, BAND + CHUNK), 1)
    for c in range(NC):
        r0 = c * CHUNK
        cg = i * NC + c
        band_lo = FIRST_VALID - (cg - WINDOW_CHUNKS) * CHUNK
        meta_lo = jnp.where(cg > NC - 1 + WINDOW_CHUNKS, BAND + CHUNK - N_META, BAND + CHUNK)
        ok = ((col >= band_lo) & (col < BAND)) | (col >= meta_lo)
        bias = jnp.where(ok, 0.0, -jnp.inf)
        for kv in range(ATT_KV_HEADS):
            qg = jnp.concatenate([q_scr[kv * ATT_GROUP + j, r0:r0 + CHUNK, :]
                                  for j in range(ATT_GROUP)], axis=0)
            kb = jnp.concatenate([kext_scr[kv, r0 + CHUNK:r0 + CHUNK + BAND, :],
                                  kext_scr[kv, 0:CHUNK, :]], axis=0)
            vb = jnp.concatenate([vext_scr[kv, r0 + CHUNK:r0 + CHUNK + BAND, :],
                                  vext_scr[kv, 0:CHUNK, :]], axis=0)
            s = lax.dot_general(qg, kb, NT_DIMS, preferred_element_type=F32) + bias
            sink = sink_cols[kv]
            m = jnp.maximum(jnp.max(s, axis=-1, keepdims=True), sink)
            p = jnp.exp(s - m)
            den = jnp.sum(p, axis=-1, keepdims=True) + jnp.exp(sink - m)
            o = _dot(p.astype(BF16), vb) / den
            for j in range(ATT_GROUP):
                oh_scr[kv * ATT_GROUP + j, r0:r0 + CHUNK, :] = o[j * CHUNK:(j + 1) * CHUNK, :]
    o_all = jnp.concatenate([oh_scr[hq] for hq in range(ATT_Q_HEADS)], axis=-1)
    zc_scr[...] = (o_all * _silu(cg_scr[...])).astype(BF16)

    for kv in range(ATT_KV_HEADS):
        for ext in (kext_scr, vext_scr):
            last = ext[kv, KV_HEAD + TILE - CHUNK:KV_HEAD + TILE, :]
            ext[kv, 0:CHUNK, :] = jnp.where(i == 0, last, ext[kv, 0:CHUNK, :])
            ext[kv, CHUNK:KV_HEAD, :] = ext[kv, TILE + CHUNK:TILE + KV_HEAD, :]
    z = _dot(zc_scr[...], wao_ref[...])
    g = jax.nn.sigmoid(_dot(hn_scr[...], wg_ref[:, 2 * D_MODEL:3 * D_MODEL]))
    mix_scr[...] += g * z

    delta = _dot(mix_scr[...].astype(BF16), wout_ref[...])

    @pl.when(i == 0)
    def _store_meta():
        om_ref[...] = hm_ref[...] + delta
        ox_ref[...] = hx_ref[...]

    @pl.when(i > 0)
    def _store():
        ox_ref[...] = hx_ref[...] + delta


def _layer_call():
    def per_layer(shape):
        return pl.BlockSpec((None,) + shape, lambda i, l: (l[0],) + (0,) * len(shape),
                            pipeline_mode=pl.Buffered(1))

    def whole(shape):
        return pl.BlockSpec(shape, lambda i, l: (0,) * len(shape), pipeline_mode=pl.Buffered(1))

    x_spec = pl.BlockSpec((TILE, D_MODEL), lambda i, l: (jnp.maximum(i - 1, 0), 0))
    m_spec = pl.BlockSpec((TILE, D_MODEL), lambda i, l: (0, 0))
    in_specs = [
        m_spec,
        x_spec,
        per_layer((1, D_MODEL)),
        per_layer((D_MODEL, N_A)),
        per_layer((D_MODEL, N_HG)),
        per_layer((D_MODEL, N_SW)),
        per_layer((D_MODEL, N_G)),
        per_layer((CONV_WIDTH, D_CONV)),
        per_layer((1, D_CONV)),
        per_layer((1, D_CONV)),
        per_layer((1, D_CONV)),
        per_layer((D_CONV, D_MODEL)),
        whole((DEPTH, D_HG)),
        per_layer((1, HG_D)),
        per_layer((D_HG, D_MODEL)),
        per_layer((1, D_ATT)),
        per_layer((1, D_KV)),
        pl.BlockSpec(memory_space=pltpu.SMEM),
        per_layer((D_ATT, D_MODEL)),
        per_layer((D_MODEL, D_MODEL)),
    ]
    scratch = [
        pltpu.VMEM((TILE, D_MODEL), BF16),
        pltpu.VMEM((TILE, D_MODEL), F32),
        pltpu.VMEM((TILE + CONV_PAD, D_CONV), F32),
        pltpu.VMEM((SUBLANES - 1, CONV_SH_ROWS, D_CONV), F32),
        pltpu.VMEM((TILE, D_CONV), F32),
        pltpu.VMEM((TILE, D_CONV), BF16),
        pltpu.VMEM((TILE, N_HG), F32),
        pltpu.VMEM((TILE, D_HG), BF16),
        pltpu.VMEM((HG_HEADS, HG_D, HG_D), F32),
        pltpu.VMEM((ATT_Q_HEADS, TILE, HEAD_DIM), BF16),
        pltpu.VMEM((ATT_KV_HEADS, KV_HEAD + TILE, HEAD_DIM), BF16),
        pltpu.VMEM((ATT_KV_HEADS, KV_HEAD + TILE, HEAD_DIM), BF16),
        pltpu.VMEM((ATT_Q_HEADS, TILE, HEAD_DIM), F32),
        pltpu.VMEM((TILE, D_ATT), F32),
        pltpu.VMEM((TILE, D_ATT), BF16),
        pltpu.VMEM((CHUNK, CHUNK), BF16),
        pltpu.VMEM((D_ATT, D_ATT), BF16),
        pltpu.VMEM((D_KV, D_KV), BF16),
        pltpu.VMEM((1, D_HG), F32),
    ]
    return pl.pallas_call(
        _layer_kernel,
        grid_spec=pltpu.PrefetchScalarGridSpec(
            num_scalar_prefetch=1,
            grid=(N_TILES + 1,),
            in_specs=in_specs,
            out_specs=[m_spec, x_spec],
            scratch_shapes=scratch,
        ),
        out_shape=[jax.ShapeDtypeStruct((TILE, D_MODEL), F32),
                   jax.ShapeDtypeStruct((SEQ, D_MODEL), F32)],
        input_output_aliases={2: 1},
        compiler_params=pltpu.CompilerParams(
            dimension_semantics=("arbitrary",),
            vmem_limit_bytes=VMEM_LIMIT_BYTES,
        ),
        name="trunk_layer",
    )


def kernel(x, meta_tokens, norm_g, w_in, conv_w, conv_b, conv_ln_g, conv_ln_b, w_conv_out,
           hg_lower_bounds, hg_norm_g, w_hg_out, q_norm_g, k_norm_g, attn_sinks, w_att_out, w_out):
    assert x.shape == (1, SEQ, D_MODEL) and w_in.shape == (DEPTH, D_MODEL, OFF_G + N_G)
    hx = x[0].astype(F32)
    hm = jnp.concatenate([jnp.zeros((FIRST_VALID, D_MODEL), F32), meta_tokens.astype(F32)], axis=0)
    w16 = w_in.astype(BF16)
    params = (
        norm_g.astype(F32)[:, None, :],
        w16[:, :, 0:N_A], w16[:, :, OFF_HG:OFF_SW], w16[:, :, OFF_SW:OFF_G], w16[:, :, OFF_G:],
        conv_w.astype(F32), conv_b.astype(F32)[:, None, :],
        conv_ln_g.astype(F32)[:, None, :], conv_ln_b.astype(F32)[:, None, :],
        w_conv_out.astype(BF16),
        hg_lower_bounds.astype(F32), hg_norm_g.astype(F32)[:, None, :], w_hg_out.astype(BF16),
        jnp.tile(q_norm_g.astype(F32), (1, ATT_Q_HEADS))[:, None, :],
        jnp.tile(k_norm_g.astype(F32), (1, ATT_KV_HEADS))[:, None, :],
        attn_sinks.astype(F32), w_att_out.astype(BF16), w_out.astype(BF16),
    )
    call = _layer_call()

    def body(l, carry):
        om, ox = call(jnp.reshape(l, (1,)).astype(jnp.int32), carry[0], carry[1], *params)
        return om, ox

    _, hx = lax.fori_loop(0, DEPTH, body, (hm, hx))
    return hx[None].astype(x.dtype)
```

```python
import math

import jax
import jax.numpy as jnp
from jax import lax
from jax.experimental import pallas as pl
from jax.experimental.pallas import tpu as pltpu

F32 = jnp.float32
BF16 = jnp.bfloat16

D_MODEL = 1024
SEQ = 16384
DEPTH = 4
CHUNK = 64
N_META = 16
D_CONV = 512
CONV_WIDTH = 31
HG_HEADS = 4
HG_D = 128
D_HG = HG_HEADS * HG_D
F_FLOOR = 1e-30
ATT_Q_HEADS = 8
ATT_KV_HEADS = 2
HEAD_DIM = 64
ATT_GROUP = ATT_Q_HEADS // ATT_KV_HEADS
D_ATT = ATT_Q_HEADS * HEAD_DIM
D_KV = ATT_KV_HEADS * HEAD_DIM
WINDOW_CHUNKS = 2
EPS = 1e-6
LOG2E = math.log2(math.e)

N_A = 2 * D_CONV + D_CONV
N_HG = 4 * D_HG
N_SW = 2 * D_ATT + 2 * D_KV
N_G = 3 * D_MODEL
OFF_HG = N_A
OFF_SW = OFF_HG + N_HG
OFF_G = OFF_SW + N_SW

TILE = 256
NC = TILE // CHUNK
N_TILES = SEQ // TILE
FIRST_VALID = TILE - N_META
SUBLANES = 8
CONV_PAD = 32
CONV_RB = 32
CONV_SH_ROWS = TILE + CONV_PAD - SUBLANES
SUB = 16
BAND = (WINDOW_CHUNKS + 1) * CHUNK
KV_HEAD = CHUNK + WINDOW_CHUNKS * CHUNK
PIECE = 512
VMEM_LIMIT_BYTES = 56 * 1024 * 1024

NT_DIMS = (((1,), (1,)), ((), ()))
TN_DIMS = (((0,), (0,)), ((), ()))


def _dot(a, b):
    return jnp.dot(a, b, preferred_element_type=F32)


def _sigmoid(x):
    return 0.5 * jnp.tanh(0.5 * x) + 0.5


def _silu(x):
    return x * _sigmoid(x)


def _hgrn_head(q_scr, k_scr, b_scr, v_ref, st_scr, r0, hd):
    ln = slice(hd * HG_D, (hd + 1) * HG_D)
    q = q_scr[r0:r0 + CHUNK, ln]
    k = k_scr[r0:r0 + CHUNK, ln]
    b = b_scr[r0:r0 + CHUNK, ln]
    v16 = v_ref[r0:r0 + CHUNK, 2 * D_HG + hd * HG_D:2 * D_HG + (hd + 1) * HG_D].astype(BF16)
    b_last = b[CHUNK - 1:CHUNK, :]
    st = st_scr[hd]
    q_dec = (q * jnp.exp2(b)).astype(BF16)
    o = lax.dot_general(q_dec, st.astype(BF16), NT_DIMS, preferred_element_type=F32)
    k_dec = (k * jnp.exp2(b_last - b)).astype(BF16)
    upd = lax.dot_general(v16, k_dec, TN_DIMS, preferred_element_type=F32)
    st_scr[hd] = st * jnp.exp2(b_last) + upd

    row = lax.broadcasted_iota(jnp.int32, (CHUNK, 1), 0)
    lane = lax.broadcasted_iota(jnp.int32, (SUBLANES, CHUNK), 1)
    srow = lax.broadcasted_iota(jnp.int32, (SUBLANES, CHUNK), 0)
    blocks = []
    for i in range(CHUNK // SUB):
        lo = i * SUB
        if i == 0:
            off = jnp.zeros((SUB, CHUNK), F32)
        else:
            b_ref = b[lo - 1:lo, :]
            qt = (q[lo:lo + SUB] * jnp.exp2(b[lo:lo + SUB] - b_ref)).astype(BF16)
            kt = jnp.where(row < lo, k * jnp.exp2(jnp.minimum(b_ref - b, 0.0)), 0.0).astype(BF16)
            off = lax.dot_general(qt, kt, NT_DIMS, preferred_element_type=F32)
        for half in range(SUB // SUBLANES):
            h0 = lo + half * SUBLANES
            qh = q[h0:h0 + SUBLANES]
            bh = b[h0:h0 + SUBLANES]
            sc = off[half * SUBLANES:(half + 1) * SUBLANES]
            for s in range((half + 1) * SUBLANES):
                ks = k_scr[r0 + lo + s:r0 + lo + s + 1, ln]
                bs = b_scr[r0 + lo + s:r0 + lo + s + 1, ln]
                col = jnp.sum(qh * ks * jnp.exp2(bh - bs), axis=-1, keepdims=True)
                keep = lane == lo + s
                if s >= half * SUBLANES:
                    keep = keep & (srow >= s - half * SUBLANES)
                sc = jnp.where(keep, col, sc)
            blocks.append(sc)
    scores = jnp.concatenate(blocks, axis=0).astype(BF16)
    return o + _dot(scores, v16)


def _layer_kernel(layer_ref, hm_ref, hx_ref, ng_ref, wa_ref, whg_ref, wsw_ref, wg_ref,
                  cw_ref, cb_ref, lng_ref, lnb_ref, wco_ref, lb_ref, hgn_ref, who_ref,
                  qn_ref, kn_ref, sinks_ref, wao_ref, wout_ref,
                  om_ref, ox_ref,
                  hn_scr, mix_scr, ext_scr, sh_scr, cwb_scr, ag_scr, za_scr, uh_scr, usw_scr, g_scr,
                  hq_scr, hk_scr, hb_scr, zb_scr, st_scr,
                  q_scr, kext_scr, vext_scr, oh_scr, zc_scr,
                  tri_scr, bdq_scr, bdk_scr, lbrow_scr):
    i = pl.program_id(0)
    layer = layer_ref[0]
    first_valid = jnp.where(i > 0, 0, FIRST_VALID)

    @pl.when(i == 0)
    def _init():
        ext_scr[0:CONV_PAD, :] = jnp.zeros((CONV_PAD, D_CONV), F32)
        st_scr[...] = jnp.zeros(st_scr.shape, F32)
        kext_scr[:, 0:KV_HEAD, :] = jnp.zeros((ATT_KV_HEADS, KV_HEAD, HEAD_DIM), BF16)
        vext_scr[:, 0:KV_HEAD, :] = jnp.zeros((ATT_KV_HEADS, KV_HEAD, HEAD_DIM), BF16)
        for j in range(CONV_WIDTH):
            cwb_scr[j] = jnp.broadcast_to(cw_ref[j:j + 1, :], (SUBLANES, D_CONV))
        r = lax.broadcasted_iota(jnp.int32, (CHUNK, CHUNK), 0)
        c = lax.broadcasted_iota(jnp.int32, (CHUNK, CHUNK), 1)
        tri_scr[...] = jnp.where(c <= r, 1.0, 0.0).astype(BF16)
        r = lax.broadcasted_iota(jnp.int32, (D_ATT, D_ATT), 0) // HEAD_DIM
        c = lax.broadcasted_iota(jnp.int32, (D_ATT, D_ATT), 1) // HEAD_DIM
        bdq_scr[...] = jnp.where(r == c, 1.0, 0.0).astype(BF16)
        r = lax.broadcasted_iota(jnp.int32, (D_KV, D_KV), 0) // HEAD_DIM
        c = lax.broadcasted_iota(jnp.int32, (D_KV, D_KV), 1) // HEAD_DIM
        bdk_scr[...] = jnp.where(r == c, 1.0, 0.0).astype(BF16)
        rows = [lb_ref[j:j + 1, :] for j in range(DEPTH)]
        mx = rows[0]
        for j in range(1, DEPTH):
            mx = jnp.maximum(mx, rows[j])
        ex = [jnp.exp(rw - mx) for rw in rows]
        tot = ex[0]
        for j in range(1, DEPTH):
            tot = tot + ex[j]
        sm = [e / tot for e in ex]
        cum = sm[0]
        lb_sel = jnp.zeros((1, D_HG), F32)
        for j in range(DEPTH):
            if j > 0:
                cum = cum + sm[j]
            lb_j = jnp.clip(cum - sm[0], 0.0, 1.0)
            lb_sel = jnp.where(layer == j, lb_j, lb_sel)
        lbrow_scr[...] = lb_sel

    def proj(dst_scr, w_ref, c0, n):
        def run():
            dst_scr[:, c0:c0 + n] = _dot(hn_scr[...], w_ref[:, c0:c0 + n])
        return run

    conv_side = ([proj(uh_scr, whg_ref, c0, PIECE) for c0 in range(0, N_HG, PIECE)]
                 + [proj(usw_scr, wsw_ref, c0, min(PIECE, N_SW - c0)) for c0 in range(0, N_SW, PIECE)])
    hgrn_side = [proj(g_scr, wg_ref, c0, PIECE) for c0 in range(0, N_G, PIECE)]

    h = jnp.where(i == 0, hm_ref[...], hx_ref[...])
    ms = jnp.mean(h * h, axis=-1, keepdims=True)
    hn_scr[...] = (h * lax.rsqrt(ms + EPS) * ng_ref[...]).astype(BF16)

    a = _dot(hn_scr[...], wa_ref[:, 0:2 * D_CONV])
    u = a[:, 0:D_CONV] * _sigmoid(a[:, D_CONV:2 * D_CONV])
    row_t = lax.broadcasted_iota(jnp.int32, (TILE, 1), 0)
    ext_scr[CONV_PAD:CONV_PAD + TILE, :] = jnp.where(row_t >= first_valid, u, 0.0)
    ag_scr[...] = _dot(hn_scr[...], wa_ref[:, 2 * D_CONV:N_A])
    for r in range(1, SUBLANES):
        sh_scr[r - 1] = ext_scr[r:r + CONV_SH_ROWS, :]
    for rb in range(TILE // CONV_RB):
        r0 = rb * CONV_RB
        acc = jnp.broadcast_to(cb_ref[...], (CONV_RB // SUBLANES, SUBLANES, D_CONV))
        for j in range(CONV_WIDTH):
            m = CONV_PAD - (CONV_WIDTH - 1) + j
            a0 = r0 + m - m % SUBLANES
            if m % SUBLANES == 0:
                src = ext_scr[a0:a0 + CONV_RB, :]
            else:
                src = sh_scr[m % SUBLANES - 1, a0:a0 + CONV_RB, :]
            acc = acc + cwb_scr[j][None] * src.reshape(CONV_RB // SUBLANES, SUBLANES, D_CONV)
        acc = acc.reshape(CONV_RB, D_CONV)
        mu = jnp.mean(acc, axis=-1, keepdims=True)
        d = acc - mu
        var = jnp.mean(d * d, axis=-1, keepdims=True)
        y = d * lax.rsqrt(var + EPS) * lng_ref[...] + lnb_ref[...]
        za_scr[r0:r0 + CONV_RB, :] = (_silu(y) * _silu(ag_scr[r0:r0 + CONV_RB, :])).astype(BF16)
        if rb < len(conv_side):
            conv_side[rb]()
    for run in conv_side[TILE // CONV_RB:]:
        run()
    ext_scr[0:CONV_PAD, :] = ext_scr[TILE:TILE + CONV_PAD, :]

    lb = lbrow_scr[...]
    tri = tri_scr[...]
    for c in range(NC):
        r0 = c * CHUNK
        rows = slice(r0, r0 + CHUNK)
        vrow = (lax.broadcasted_iota(jnp.int32, (CHUNK, 1), 0) + r0) >= first_valid
        zf = uh_scr[rows, D_HG:2 * D_HG]
        f = lb + (1.0 - lb) * jax.nn.sigmoid(zf)
        logf = jnp.where(vrow, jnp.log(jnp.maximum(f, F_FLOOR)), 0.0)
        hk_scr[rows, :] = jnp.where(vrow, (1.0 - lb) * jax.nn.sigmoid(-zf), 0.0)
        p0 = logf.astype(BF16)
        r1 = logf - p0.astype(F32)
        p1 = r1.astype(BF16)
        p2 = (r1 - p1.astype(F32)).astype(BF16)
        hb_scr[rows, :] = (_dot(tri, p0) + _dot(tri, p1) + _dot(tri, p2)) * LOG2E
        hq_scr[rows, :] = _silu(uh_scr[rows, 0:D_HG])
    side = list(hgrn_side)
    for c in range(NC):
        r0 = c * CHUNK
        rows = slice(r0, r0 + CHUNK)
        outs = []
        for hd in range(HG_HEADS):
            o = _hgrn_head(hq_scr, hk_scr, hb_scr, uh_scr, st_scr, r0, hd)
            o = o * lax.rsqrt(jnp.mean(o * o, axis=-1, keepdims=True) + EPS) * hgn_ref[...]
            outs.append(o)
            if side and hd % 2 == 1:
                side.pop(0)()
        o_all = jnp.concatenate(outs, axis=-1)
        zb_scr[rows, :] = (o_all * _silu(uh_scr[rows, 3 * D_HG:4 * D_HG])).astype(BF16)
    for run in side:
        run()
    mix_scr[...] = _sigmoid(g_scr[:, 0:D_MODEL]) * _dot(za_scr[...], wco_ref[...])

    q = usw_scr[:, 0:D_ATT]
    ssq = _dot((q * q).astype(BF16), bdq_scr[...])
    qn = (q * lax.rsqrt(ssq * (1.0 / HEAD_DIM) + EPS)
          * (qn_ref[...] * (HEAD_DIM ** -0.5))).astype(BF16)
    for hq in range(ATT_Q_HEADS):
        q_scr[hq] = qn[:, hq * HEAD_DIM:(hq + 1) * HEAD_DIM]
    k = usw_scr[:, D_ATT:D_ATT + D_KV]
    ssk = _dot((k * k).astype(BF16), bdk_scr[...])
    kn = (k * lax.rsqrt(ssk * (1.0 / HEAD_DIM) + EPS) * kn_ref[...]).astype(BF16)
    vv = usw_scr[:, D_ATT + D_KV:D_ATT + 2 * D_KV].astype(BF16)
    for kv in range(ATT_KV_HEADS):
        kext_scr[kv, KV_HEAD:KV_HEAD + TILE, :] = kn[:, kv * HEAD_DIM:(kv + 1) * HEAD_DIM]
        vext_scr[kv, KV_HEAD:KV_HEAD + TILE, :] = vv[:, kv * HEAD_DIM:(kv + 1) * HEAD_DIM]

    grow = lax.broadcasted_iota(jnp.int32, (ATT_GROUP * CHUNK, 1), 0)
    sink_cols = []
    for kv in range(ATT_KV_HEADS):
        sc = jnp.zeros((ATT_GROUP * CHUNK, 1), F32)
        for j in range(ATT_GROUP):
            sc = jnp.where(grow // CHUNK == j, sinks_ref[layer, kv * ATT_GROUP + j], sc)
        sink_cols.append(sc)
    col = lax.broadcasted_iota(jnp.int32, (1, BAND + CHUNK), 1)
    for c in range(NC):
        r0 = c * CHUNK
        cg = i * NC + c
        band_lo = FIRST_VALID - (cg - WINDOW_CHUNKS) * CHUNK
        meta_lo = jnp.where(cg > NC - 1 + WINDOW_CHUNKS, BAND + CHUNK - N_META, BAND + CHUNK)
        ok = ((col >= band_lo) & (col < BAND)) | (col >= meta_lo)
        bias = jnp.where(ok, 0.0, -jnp.inf)
        for kv in range(ATT_KV_HEADS):
            qg = jnp.concatenate([q_scr[kv * ATT_GROUP + j, r0:r0 + CHUNK, :]
                                  for j in range(ATT_GROUP)], axis=0)
            kb = jnp.concatenate([kext_scr[kv, r0 + CHUNK:r0 + CHUNK + BAND, :],
                                  kext_scr[kv, 0:CHUNK, :]], axis=0)
            vb = jnp.concatenate([vext_scr[kv, r0 + CHUNK:r0 + CHUNK + BAND, :],
                                  vext_scr[kv, 0:CHUNK, :]], axis=0)
            s = lax.dot_general(qg, kb, NT_DIMS, preferred_element_type=F32) + bias
            sink = sink_cols[kv]
            m = jnp.maximum(jnp.max(s, axis=-1, keepdims=True), sink)
            p = jnp.exp(s - m)
            den = jnp.sum(p, axis=-1, keepdims=True) + jnp.exp(sink - m)
            o = _dot(p.astype(BF16), vb) / den
            for j in range(ATT_GROUP):
                oh_scr[kv * ATT_GROUP + j, r0:r0 + CHUNK, :] = o[j * CHUNK:(j + 1) * CHUNK, :]
        if c == 0:
            mix_scr[...] += _sigmoid(g_scr[:, D_MODEL:2 * D_MODEL]) * _dot(zb_scr[...], who_ref[...])
    o_all = jnp.concatenate([oh_scr[hq] for hq in range(ATT_Q_HEADS)], axis=-1)
    zc_scr[...] = (o_all * _silu(usw_scr[:, D_ATT + 2 * D_KV:N_SW])).astype(BF16)

    for kv in range(ATT_KV_HEADS):
        for ext in (kext_scr, vext_scr):
            last = ext[kv, KV_HEAD + TILE - CHUNK:KV_HEAD + TILE, :]
            ext[kv, 0:CHUNK, :] = jnp.where(i == 0, last, ext[kv, 0:CHUNK, :])
            ext[kv, CHUNK:KV_HEAD, :] = ext[kv, TILE + CHUNK:TILE + KV_HEAD, :]

    mixed = mix_scr[...] + _sigmoid(g_scr[:, 2 * D_MODEL:3 * D_MODEL]) * _dot(zc_scr[...], wao_ref[...])
    delta = _dot(mixed.astype(BF16), wout_ref[...])

    @pl.when(i == 0)
    def _store_meta():
        om_ref[...] = hm_ref[...] + delta
        ox_ref[...] = hx_ref[...]

    @pl.when(i > 0)
    def _store():
        ox_ref[...] = hx_ref[...] + delta


def _layer_call():
    def per_layer(shape):
        return pl.BlockSpec((None,) + shape, lambda i, l: (l[0],) + (0,) * len(shape),
                            pipeline_mode=pl.Buffered(1))

    def whole(shape):
        return pl.BlockSpec(shape, lambda i, l: (0,) * len(shape), pipeline_mode=pl.Buffered(1))

    x_spec = pl.BlockSpec((TILE, D_MODEL), lambda i, l: (jnp.maximum(i - 1, 0), 0))
    m_spec = pl.BlockSpec((TILE, D_MODEL), lambda i, l: (0, 0))
    in_specs = [
        m_spec,
        x_spec,
        per_layer((1, D_MODEL)),
        per_layer((D_MODEL, N_A)),
        per_layer((D_MODEL, N_HG)),
        per_layer((D_MODEL, N_SW)),
        per_layer((D_MODEL, N_G)),
        per_layer((CONV_WIDTH, D_CONV)),
        per_layer((1, D_CONV)),
        per_layer((1, D_CONV)),
        per_layer((1, D_CONV)),
        per_layer((D_CONV, D_MODEL)),
        whole((DEPTH, D_HG)),
        per_layer((1, HG_D)),
        per_layer((D_HG, D_MODEL)),
        per_layer((1, D_ATT)),
        per_layer((1, D_KV)),
        pl.BlockSpec(memory_space=pltpu.SMEM),
        per_layer((D_ATT, D_MODEL)),
        per_layer((D_MODEL, D_MODEL)),
    ]
    scratch = [
        pltpu.VMEM((TILE, D_MODEL), BF16),
        pltpu.VMEM((TILE, D_MODEL), F32),
        pltpu.VMEM((TILE + CONV_PAD, D_CONV), F32),
        pltpu.VMEM((SUBLANES - 1, CONV_SH_ROWS, D_CONV), F32),
        pltpu.VMEM((CONV_WIDTH, SUBLANES, D_CONV), F32),
        pltpu.VMEM((TILE, D_CONV), F32),
        pltpu.VMEM((TILE, D_CONV), BF16),
        pltpu.VMEM((TILE, N_HG), F32),
        pltpu.VMEM((TILE, N_SW), F32),
        pltpu.VMEM((TILE, N_G), F32),
        pltpu.VMEM((TILE, D_HG), F32),
        pltpu.VMEM((TILE, D_HG), F32),
        pltpu.VMEM((TILE, D_HG), F32),
        pltpu.VMEM((TILE, D_HG), BF16),
        pltpu.VMEM((HG_HEADS, HG_D, HG_D), F32),
        pltpu.VMEM((ATT_Q_HEADS, TILE, HEAD_DIM), BF16),
        pltpu.VMEM((ATT_KV_HEADS, KV_HEAD + TILE, HEAD_DIM), BF16),
        pltpu.VMEM((ATT_KV_HEADS, KV_HEAD + TILE, HEAD_DIM), BF16),
        pltpu.VMEM((ATT_Q_HEADS, TILE, HEAD_DIM), F32),
        pltpu.VMEM((TILE, D_ATT), BF16),
        pltpu.VMEM((CHUNK, CHUNK), BF16),
        pltpu.VMEM((D_ATT, D_ATT), BF16),
        pltpu.VMEM((D_KV, D_KV), BF16),
        pltpu.VMEM((1, D_HG), F32),
    ]
    return pl.pallas_call(
        _layer_kernel,
        grid_spec=pltpu.PrefetchScalarGridSpec(
            num_scalar_prefetch=1,
            grid=(N_TILES + 1,),
            in_specs=in_specs,
            out_specs=[m_spec, x_spec],
            scratch_shapes=scratch,
        ),
        out_shape=[jax.ShapeDtypeStruct((TILE, D_MODEL), F32),
                   jax.ShapeDtypeStruct((SEQ, D_MODEL), F32)],
        input_output_aliases={2: 1},
        compiler_params=pltpu.CompilerParams(
            dimension_semantics=("arbitrary",),
            vmem_limit_bytes=VMEM_LIMIT_BYTES,
        ),
        name="trunk_layer",
    )


def kernel(x, meta_tokens, norm_g, w_in, conv_w, conv_b, conv_ln_g, conv_ln_b, w_conv_out,
           hg_lower_bounds, hg_norm_g, w_hg_out, q_norm_g, k_norm_g, attn_sinks, w_att_out, w_out):
    assert x.shape == (1, SEQ, D_MODEL) and w_in.shape == (DEPTH, D_MODEL, OFF_G + N_G)
    hx = x[0].astype(F32)
    hm = jnp.concatenate([jnp.zeros((FIRST_VALID, D_MODEL), F32), meta_tokens.astype(F32)], axis=0)
    w16 = w_in.astype(BF16)
    params = (
        norm_g.astype(F32)[:, None, :],
        w16[:, :, 0:N_A], w16[:, :, OFF_HG:OFF_SW], w16[:, :, OFF_SW:OFF_G], w16[:, :, OFF_G:],
        conv_w.astype(F32), conv_b.astype(F32)[:, None, :],
        conv_ln_g.astype(F32)[:, None, :], conv_ln_b.astype(F32)[:, None, :],
        w_conv_out.astype(BF16),
        hg_lower_bounds.astype(F32), hg_norm_g.astype(F32)[:, None, :], w_hg_out.astype(BF16),
        jnp.tile(q_norm_g.astype(F32), (1, ATT_Q_HEADS))[:, None, :],
        jnp.tile(k_norm_g.astype(F32), (1, ATT_KV_HEADS))[:, None, :],
        attn_sinks.astype(F32), w_att_out.astype(BF16), w_out.astype(BF16),
    )
    call = _layer_call()

    def body(l, carry):
        om, ox = call(jnp.reshape(l, (1,)).astype(jnp.int32), carry[0], carry[1], *params)
        return om, ox

    _, hx = lax.fori_loop(0, DEPTH, body, (hm, hx))
    return hx[None].astype(x.dtype)
```

```python
import math

import jax
import jax.numpy as jnp
from jax import lax
from jax.experimental import pallas as pl
from jax.experimental.pallas import tpu as pltpu

F32 = jnp.float32
BF16 = jnp.bfloat16

D_MODEL = 1024
SEQ = 16384
DEPTH = 4
CHUNK = 64
N_META = 16
D_CONV = 512
CONV_WIDTH = 31
HG_HEADS = 4
HG_D = 128
D_HG = HG_HEADS * HG_D
F_FLOOR = 1e-30
ATT_Q_HEADS = 8
ATT_KV_HEADS = 2
HEAD_DIM = 64
ATT_GROUP = ATT_Q_HEADS // ATT_KV_HEADS
D_ATT = ATT_Q_HEADS * HEAD_DIM
D_KV = ATT_KV_HEADS * HEAD_DIM
WINDOW_CHUNKS = 2
EPS = 1e-6
LOG2E = math.log2(math.e)

N_A = 2 * D_CONV + D_CONV
N_HG = 4 * D_HG
N_SW = 2 * D_ATT + 2 * D_KV
N_G = 3 * D_MODEL
OFF_HG = N_A
OFF_SW = OFF_HG + N_HG
OFF_G = OFF_SW + N_SW
D_IN = OFF_G + N_G

TILE = 256
NC = TILE // CHUNK
N_TILES = SEQ // TILE
FIRST_VALID = TILE - N_META
SUBLANES = 8
CONV_PAD = 32
CONV_RB = 32
CONV_SH_ROWS = TILE + CONV_PAD - SUBLANES
SUB = 16
BAND = (WINDOW_CHUNKS + 1) * CHUNK
KV_HEAD = CHUNK + WINDOW_CHUNKS * CHUNK
PIECE = 512
VMEM_LIMIT_BYTES = 56 * 1024 * 1024

NT_DIMS = (((1,), (1,)), ((), ()))
TN_DIMS = (((0,), (0,)), ((), ()))


def _dot(a, b):
    return jnp.dot(a, b, preferred_element_type=F32)


def _sigmoid(x):
    return 0.5 * jnp.tanh(0.5 * x) + 0.5


def _silu(x):
    hx = 0.5 * x
    return hx * jnp.tanh(hx) + hx


def _gated(g, z):
    return z + jnp.tanh(0.5 * g) * z


def _hgrn_head(q_scr, k_scr, b_scr, v_ref, st_scr, r0, hd):
    ln = slice(hd * HG_D, (hd + 1) * HG_D)
    q = q_scr[r0:r0 + CHUNK, ln]
    k = k_scr[r0:r0 + CHUNK, ln]
    b = b_scr[r0:r0 + CHUNK, ln]
    v16 = v_ref[r0:r0 + CHUNK, 2 * D_HG + hd * HG_D:2 * D_HG + (hd + 1) * HG_D].astype(BF16)
    b_last = b[CHUNK - 1:CHUNK, :]
    st = st_scr[hd]
    q_dec = (q * jnp.exp2(b)).astype(BF16)
    o = lax.dot_general(q_dec, st.astype(BF16), NT_DIMS, preferred_element_type=F32)
    k_dec = (k * jnp.exp2(b_last - b)).astype(BF16)
    upd = lax.dot_general(v16, k_dec, TN_DIMS, preferred_element_type=F32)
    st_scr[hd] = st * jnp.exp2(b_last) + upd

    row = lax.broadcasted_iota(jnp.int32, (CHUNK, 1), 0)
    lane = lax.broadcasted_iota(jnp.int32, (SUBLANES, CHUNK), 1)
    srow = lax.broadcasted_iota(jnp.int32, (SUBLANES, CHUNK), 0)
    blocks = []
    for i in range(CHUNK // SUB):
        lo = i * SUB
        if i == 0:
            off = jnp.zeros((SUB, CHUNK), F32)
        else:
            b_ref = b[lo - 1:lo, :]
            qt = (q[lo:lo + SUB] * jnp.exp2(b[lo:lo + SUB] - b_ref)).astype(BF16)
            kt = jnp.where(row < lo, k * jnp.exp2(jnp.minimum(b_ref - b, 0.0)), 0.0).astype(BF16)
            off = lax.dot_general(qt, kt, NT_DIMS, preferred_element_type=F32)
        for half in range(SUB // SUBLANES):
            h0 = lo + half * SUBLANES
            qh = q[h0:h0 + SUBLANES]
            bh = b[h0:h0 + SUBLANES]
            sc = off[half * SUBLANES:(half + 1) * SUBLANES]
            for s in range((half + 1) * SUBLANES):
                ks = k_scr[r0 + lo + s:r0 + lo + s + 1, ln]
                bs = b_scr[r0 + lo + s:r0 + lo + s + 1, ln]
                col = jnp.sum(qh * ks * jnp.exp2(bh - bs), axis=-1, keepdims=True)
                keep = lane == lo + s
                if s >= half * SUBLANES:
                    keep = keep & (srow >= s - half * SUBLANES)
                sc = jnp.where(keep, col, sc)
            blocks.append(sc)
    scores = jnp.concatenate(blocks, axis=0).astype(BF16)
    return o + _dot(scores, v16)


def _layer_kernel(layer_ref, hm_ref, hx_ref, ng_ref, win_ref,
                  cw_ref, cb_ref, lng_ref, lnb_ref, wco_ref, lb_ref, hgn_ref, who_ref,
                  qn_ref, kn_ref, sinks_ref, wao_ref, wout_ref,
                  om_ref, ox_ref,
                  hn_scr, mix_scr, ext_scr, sh_scr, cwb_scr, ag_scr, za_scr, uh_scr, usw_scr, g_scr,
                  hq_scr, hk_scr, hb_scr, zb_scr, st_scr,
                  q_scr, kext_scr, vext_scr, oh_scr, zc_scr,
                  tri_scr, bdq_scr, bdk_scr, lbrow_scr):
    i = pl.program_id(0)
    layer = layer_ref[0]
    first_valid = jnp.where(i > 0, 0, FIRST_VALID)

    @pl.when(i == 0)
    def _init():
        ext_scr[0:CONV_PAD, :] = jnp.zeros((CONV_PAD, D_CONV), F32)
        st_scr[...] = jnp.zeros(st_scr.shape, F32)
        kext_scr[:, 0:KV_HEAD, :] = jnp.zeros((ATT_KV_HEADS, KV_HEAD, HEAD_DIM), BF16)
        vext_scr[:, 0:KV_HEAD, :] = jnp.zeros((ATT_KV_HEADS, KV_HEAD, HEAD_DIM), BF16)
        for j in range(CONV_WIDTH):
            cwb_scr[j] = jnp.broadcast_to(cw_ref[j:j + 1, :], (SUBLANES, D_CONV))
        r = lax.broadcasted_iota(jnp.int32, (CHUNK, CHUNK), 0)
        c = lax.broadcasted_iota(jnp.int32, (CHUNK, CHUNK), 1)
        tri_scr[...] = jnp.where(c <= r, 1.0, 0.0).astype(BF16)
        r = lax.broadcasted_iota(jnp.int32, (D_ATT, D_ATT), 0) // HEAD_DIM
        c = lax.broadcasted_iota(jnp.int32, (D_ATT, D_ATT), 1) // HEAD_DIM
        bdq_scr[...] = jnp.where(r == c, 1.0, 0.0).astype(BF16)
        r = lax.broadcasted_iota(jnp.int32, (D_KV, D_KV), 0) // HEAD_DIM
        c = lax.broadcasted_iota(jnp.int32, (D_KV, D_KV), 1) // HEAD_DIM
        bdk_scr[...] = jnp.where(r == c, 1.0, 0.0).astype(BF16)
        rows = [lb_ref[j:j + 1, :] for j in range(DEPTH)]
        mx = rows[0]
        for j in range(1, DEPTH):
            mx = jnp.maximum(mx, rows[j])
        ex = [jnp.exp(rw - mx) for rw in rows]
        tot = ex[0]
        for j in range(1, DEPTH):
            tot = tot + ex[j]
        sm = [e / tot for e in ex]
        cum = sm[0]
        lb_sel = jnp.zeros((1, D_HG), F32)
        for j in range(DEPTH):
            if j > 0:
                cum = cum + sm[j]
            lb_j = jnp.clip(cum - sm[0], 0.0, 1.0)
            lb_sel = jnp.where(layer == j, lb_j, lb_sel)
        lbrow_scr[...] = lb_sel

    def proj(dst_scr, w0, c0, n):
        def run():
            dst_scr[:, c0:c0 + n] = _dot(hn_scr[...], win_ref[:, w0 + c0:w0 + c0 + n])
        return run

    conv_side = ([proj(uh_scr, OFF_HG, c0, PIECE) for c0 in range(0, N_HG, PIECE)]
                 + [proj(usw_scr, OFF_SW, c0, min(PIECE, N_SW - c0)) for c0 in range(0, N_SW, PIECE)])
    hgrn_side = [proj(g_scr, OFF_G, c0, PIECE) for c0 in range(0, N_G, PIECE)]

    h = jnp.where(i == 0, hm_ref[...], hx_ref[...])
    ms = jnp.mean(h * h, axis=-1, keepdims=True)
    hn_scr[...] = (h * lax.rsqrt(ms + EPS) * ng_ref[...]).astype(BF16)

    a = _dot(hn_scr[...], win_ref[:, 0:2 * D_CONV])
    u = a[:, 0:D_CONV] * _sigmoid(a[:, D_CONV:2 * D_CONV])
    row_t = lax.broadcasted_iota(jnp.int32, (TILE, 1), 0)
    ext_scr[CONV_PAD:CONV_PAD + TILE, :] = jnp.where(row_t >= first_valid, u, 0.0)
    ag_scr[...] = _dot(hn_scr[...], win_ref[:, 2 * D_CONV:N_A])
    for r in range(1, SUBLANES):
        sh_scr[r - 1] = ext_scr[r:r + CONV_SH_ROWS, :]
    for rb in range(TILE // CONV_RB):
        r0 = rb * CONV_RB
        acc = jnp.broadcast_to(cb_ref[...], (CONV_RB // SUBLANES, SUBLANES, D_CONV))
        for j in range(CONV_WIDTH):
            m = CONV_PAD - (CONV_WIDTH - 1) + j
            a0 = r0 + m - m % SUBLANES
            if m % SUBLANES == 0:
                src = ext_scr[a0:a0 + CONV_RB, :]
            else:
                src = sh_scr[m % SUBLANES - 1, a0:a0 + CONV_RB, :]
            acc = acc + cwb_scr[j][None] * src.reshape(CONV_RB // SUBLANES, SUBLANES, D_CONV)
        acc = acc.reshape(CONV_RB, D_CONV)
        mu = jnp.mean(acc, axis=-1, keepdims=True)
        d = acc - mu
        var = jnp.mean(d * d, axis=-1, keepdims=True)
        y = d * lax.rsqrt(var + EPS) * lng_ref[...] + lnb_ref[...]
        za_scr[r0:r0 + CONV_RB, :] = (_silu(y) * _silu(ag_scr[r0:r0 + CONV_RB, :])).astype(BF16)
        if rb < len(conv_side):
            conv_side[rb]()
    for run in conv_side[TILE // CONV_RB:]:
        run()
    ext_scr[0:CONV_PAD, :] = ext_scr[TILE:TILE + CONV_PAD, :]

    lb = lbrow_scr[...]
    tri = tri_scr[...]
    for c in range(NC):
        r0 = c * CHUNK
        rows = slice(r0, r0 + CHUNK)
        vrow = (lax.broadcasted_iota(jnp.int32, (CHUNK, 1), 0) + r0) >= first_valid
        zf = uh_scr[rows, D_HG:2 * D_HG]
        f = lb + (1.0 - lb) * jax.nn.sigmoid(zf)
        logf = jnp.where(vrow, jnp.log(jnp.maximum(f, F_FLOOR)), 0.0)
        hk_scr[rows, :] = jnp.where(vrow, (1.0 - lb) * jax.nn.sigmoid(-zf), 0.0)
        p0 = logf.astype(BF16)
        r1 = logf - p0.astype(F32)
        p1 = r1.astype(BF16)
        p2 = (r1 - p1.astype(F32)).astype(BF16)
        hb_scr[rows, :] = (_dot(tri, p0) + _dot(tri, p1) + _dot(tri, p2)) * LOG2E
        hq_scr[rows, :] = _silu(uh_scr[rows, 0:D_HG])
    side = list(hgrn_side)
    for c in range(NC):
        r0 = c * CHUNK
        rows = slice(r0, r0 + CHUNK)
        outs = []
        for hd in range(HG_HEADS):
            o = _hgrn_head(hq_scr, hk_scr, hb_scr, uh_scr, st_scr, r0, hd)
            o = o * lax.rsqrt(jnp.mean(o * o, axis=-1, keepdims=True) + EPS) * hgn_ref[...]
            outs.append(o)
            if side and hd % 2 == 1:
                side.pop(0)()
        o_all = jnp.concatenate(outs, axis=-1)
        zb_scr[rows, :] = (o_all * _silu(uh_scr[rows, 3 * D_HG:4 * D_HG])).astype(BF16)
    for run in side:
        run()
    mix_scr[...] = _gated(g_scr[:, 0:D_MODEL], _dot(za_scr[...], wco_ref[...]))

    q = usw_scr[:, 0:D_ATT]
    ssq = _dot((q * q).astype(BF16), bdq_scr[...])
    qn = (q * lax.rsqrt(ssq * (1.0 / HEAD_DIM) + EPS)
          * (qn_ref[...] * (HEAD_DIM ** -0.5))).astype(BF16)
    for hq in range(ATT_Q_HEADS):
        q_scr[hq] = qn[:, hq * HEAD_DIM:(hq + 1) * HEAD_DIM]
    k = usw_scr[:, D_ATT:D_ATT + D_KV]
    ssk = _dot((k * k).astype(BF16), bdk_scr[...])
    kn = (k * lax.rsqrt(ssk * (1.0 / HEAD_DIM) + EPS) * kn_ref[...]).astype(BF16)
    vv = usw_scr[:, D_ATT + D_KV:D_ATT + 2 * D_KV].astype(BF16)
    for kv in range(ATT_KV_HEADS):
        kext_scr[kv, KV_HEAD:KV_HEAD + TILE, :] = kn[:, kv * HEAD_DIM:(kv + 1) * HEAD_DIM]
        vext_scr[kv, KV_HEAD:KV_HEAD + TILE, :] = vv[:, kv * HEAD_DIM:(kv + 1) * HEAD_DIM]

    grow = lax.broadcasted_iota(jnp.int32, (ATT_GROUP * CHUNK, 1), 0)
    sink_cols = []
    for kv in range(ATT_KV_HEADS):
        sc = jnp.zeros((ATT_GROUP * CHUNK, 1), F32)
        for j in range(ATT_GROUP):
            sc = jnp.where(grow // CHUNK == j, sinks_ref[layer, kv * ATT_GROUP + j], sc)
        sink_cols.append(sc)
    col = lax.broadcasted_iota(jnp.int32, (1, BAND + CHUNK), 1)
    for c in range(NC):
        r0 = c * CHUNK
        cg = i * NC + c
        band_lo = FIRST_VALID - (cg - WINDOW_CHUNKS) * CHUNK
        meta_lo = jnp.where(cg > NC - 1 + WINDOW_CHUNKS, BAND + CHUNK - N_META, BAND + CHUNK)
        ok = ((col >= band_lo) & (col < BAND)) | (col >= meta_lo)
        bias = jnp.where(ok, 0.0, -jnp.inf)
        for kv in range(ATT_KV_HEADS):
            qg = jnp.concatenate([q_scr[kv * ATT_GROUP + j, r0:r0 + CHUNK, :]
                                  for j in range(ATT_GROUP)], axis=0)
            kb = jnp.concatenate([kext_scr[kv, r0 + CHUNK:r0 + CHUNK + BAND, :],
                                  kext_scr[kv, 0:CHUNK, :]], axis=0)
            vb = jnp.concatenate([vext_scr[kv, r0 + CHUNK:r0 + CHUNK + BAND, :],
                                  vext_scr[kv, 0:CHUNK, :]], axis=0)
            s = lax.dot_general(qg, kb, NT_DIMS, preferred_element_type=F32) + bias
            sink = sink_cols[kv]
            m = jnp.maximum(jnp.max(s, axis=-1, keepdims=True), sink)
            p = jnp.exp(s - m)
            den = jnp.sum(p, axis=-1, keepdims=True) + jnp.exp(sink - m)
            o = _dot(p.astype(BF16), vb) / den
            for j in range(ATT_GROUP):
                oh_scr[kv * ATT_GROUP + j, r0:r0 + CHUNK, :] = o[j * CHUNK:(j + 1) * CHUNK, :]
        if c == 0:
            mix_scr[...] += _gated(g_scr[:, D_MODEL:2 * D_MODEL], _dot(zb_scr[...], who_ref[...]))
    o_all = jnp.concatenate([oh_scr[hq] for hq in range(ATT_Q_HEADS)], axis=-1)
    zc_scr[...] = (o_all * _silu(usw_scr[:, D_ATT + 2 * D_KV:N_SW])).astype(BF16)

    for kv in range(ATT_KV_HEADS):
        for ext in (kext_scr, vext_scr):
            last = ext[kv, KV_HEAD + TILE - CHUNK:KV_HEAD + TILE, :]
            ext[kv, 0:CHUNK, :] = jnp.where(i == 0, last, ext[kv, 0:CHUNK, :])
            ext[kv, CHUNK:KV_HEAD, :] = ext[kv, TILE + CHUNK:TILE + KV_HEAD, :]

    mixed = 0.5 * (mix_scr[...] + _gated(g_scr[:, 2 * D_MODEL:3 * D_MODEL], _dot(zc_scr[...], wao_ref[...])))
    delta = _dot(mixed.astype(BF16), wout_ref[...])

    @pl.when(i == 0)
    def _store_meta():
        om_ref[...] = hm_ref[...] + delta
        ox_ref[...] = hx_ref[...]

    @pl.when(i > 0)
    def _store():
        ox_ref[...] = hx_ref[...] + delta


def _layer_call():
    def per_layer(shape):
        return pl.BlockSpec((None,) + shape, lambda i, l: (l[0],) + (0,) * len(shape),
                            pipeline_mode=pl.Buffered(1))

    def whole(shape):
        return pl.BlockSpec(shape, lambda i, l: (0,) * len(shape), pipeline_mode=pl.Buffered(1))

    x_spec = pl.BlockSpec((TILE, D_MODEL), lambda i, l: (jnp.maximum(i - 1, 0), 0))
    m_spec = pl.BlockSpec((TILE, D_MODEL), lambda i, l: (0, 0))
    in_specs = [
        m_spec,
        x_spec,
        per_layer((1, D_MODEL)),
        per_layer((D_MODEL, D_IN)),
        per_layer((CONV_WIDTH, D_CONV)),
        per_layer((1, D_CONV)),
        per_layer((1, D_CONV)),
        per_layer((1, D_CONV)),
        per_layer((D_CONV, D_MODEL)),
        whole((DEPTH, D_HG)),
        per_layer((1, HG_D)),
        per_layer((D_HG, D_MODEL)),
        per_layer((1, D_ATT)),
        per_layer((1, D_KV)),
        pl.BlockSpec(memory_space=pltpu.SMEM),
        per_layer((D_ATT, D_MODEL)),
        per_layer((D_MODEL, D_MODEL)),
    ]
    scratch = [
        pltpu.VMEM((TILE, D_MODEL), BF16),
        pltpu.VMEM((TILE, D_MODEL), F32),
        pltpu.VMEM((TILE + CONV_PAD, D_CONV), F32),
        pltpu.VMEM((SUBLANES - 1, CONV_SH_ROWS, D_CONV), F32),
        pltpu.VMEM((CONV_WIDTH, SUBLANES, D_CONV), F32),
        pltpu.VMEM((TILE, D_CONV), F32),
        pltpu.VMEM((TILE, D_CONV), BF16),
        pltpu.VMEM((TILE, N_HG), F32),
        pltpu.VMEM((TILE, N_SW), F32),
        pltpu.VMEM((TILE, N_G), F32),
        pltpu.VMEM((TILE, D_HG), F32),
        pltpu.VMEM((TILE, D_HG), F32),
        pltpu.VMEM((TILE, D_HG), F32),
        pltpu.VMEM((TILE, D_HG), BF16),
        pltpu.VMEM((HG_HEADS, HG_D, HG_D), F32),
        pltpu.VMEM((ATT_Q_HEADS, TILE, HEAD_DIM), BF16),
        pltpu.VMEM((ATT_KV_HEADS, KV_HEAD + TILE, HEAD_DIM), BF16),
        pltpu.VMEM((ATT_KV_HEADS, KV_HEAD + TILE, HEAD_DIM), BF16),
        pltpu.VMEM((ATT_Q_HEADS, TILE, HEAD_DIM), F32),
        pltpu.VMEM((TILE, D_ATT), BF16),
        pltpu.VMEM((CHUNK, CHUNK), BF16),
        pltpu.VMEM((D_ATT, D_ATT), BF16),
        pltpu.VMEM((D_KV, D_KV), BF16),
        pltpu.VMEM((1, D_HG), F32),
    ]
    return pl.pallas_call(
        _layer_kernel,
        grid_spec=pltpu.PrefetchScalarGridSpec(
            num_scalar_prefetch=1,
            grid=(N_TILES + 1,),
            in_specs=in_specs,
            out_specs=[m_spec, x_spec],
            scratch_shapes=scratch,
        ),
        out_shape=[jax.ShapeDtypeStruct((TILE, D_MODEL), F32),
                   jax.ShapeDtypeStruct((SEQ, D_MODEL), F32)],
        compiler_params=pltpu.CompilerParams(
            dimension_semantics=("arbitrary",),
            vmem_limit_bytes=VMEM_LIMIT_BYTES,
        ),
        name="trunk_layer",
    )


def kernel(x, meta_tokens, norm_g, w_in, conv_w, conv_b, conv_ln_g, conv_ln_b, w_conv_out,
           hg_lower_bounds, hg_norm_g, w_hg_out, q_norm_g, k_norm_g, attn_sinks, w_att_out, w_out):
    assert x.shape == (1, SEQ, D_MODEL) and w_in.shape == (DEPTH, D_MODEL, D_IN)
    hx = x[0].astype(F32)
    hm = jnp.concatenate([jnp.zeros((FIRST_VALID, D_MODEL), F32), meta_tokens.astype(F32)], axis=0)
    params = (
        norm_g.astype(F32)[:, None, :], w_in.astype(BF16),
        conv_w.astype(F32), conv_b.astype(F32)[:, None, :],
        conv_ln_g.astype(F32)[:, None, :], conv_ln_b.astype(F32)[:, None, :],
        w_conv_out.astype(BF16),
        hg_lower_bounds.astype(F32), hg_norm_g.astype(F32)[:, None, :], w_hg_out.astype(BF16),
        jnp.tile(q_norm_g.astype(F32), (1, ATT_Q_HEADS))[:, None, :],
        jnp.tile(k_norm_g.astype(F32), (1, ATT_KV_HEADS))[:, None, :],
        attn_sinks.astype(F32), w_att_out.astype(BF16), w_out.astype(BF16),
    )
    call = _layer_call()
    for layer in range(DEPTH):
        hm, hx = call(jnp.full((1,), layer, jnp.int32), hm, hx, *params)
    return hx[None].astype(x.dtype)
```

```python
import math

import jax
import jax.numpy as jnp
from jax import lax
from jax.experimental import pallas as pl
from jax.experimental.pallas import tpu as pltpu

F32 = jnp.float32
BF16 = jnp.bfloat16

D_MODEL = 1024
SEQ = 16384
DEPTH = 4
CHUNK = 64
N_META = 16
D_CONV = 512
CONV_WIDTH = 31
HG_HEADS = 4
HG_D = 128
D_HG = HG_HEADS * HG_D
F_FLOOR = 1e-30
ATT_Q_HEADS = 8
ATT_KV_HEADS = 2
HEAD_DIM = 64
ATT_GROUP = ATT_Q_HEADS // ATT_KV_HEADS
D_ATT = ATT_Q_HEADS * HEAD_DIM
D_KV = ATT_KV_HEADS * HEAD_DIM
WINDOW_CHUNKS = 2
EPS = 1e-6
LOG2E = math.log2(math.e)

N_A = 2 * D_CONV + D_CONV
N_HG = 4 * D_HG
N_SW = 2 * D_ATT + 2 * D_KV
N_G = 3 * D_MODEL
OFF_HG = N_A
OFF_SW = OFF_HG + N_HG
OFF_G = OFF_SW + N_SW
D_IN = OFF_G + N_G

TILE = 256
NC = TILE // CHUNK
N_TILES = SEQ // TILE
FIRST_VALID = TILE - N_META
SUBLANES = 8
CONV_PAD = 32
CONV_RB = 32
CONV_SH_ROWS = TILE + CONV_PAD - SUBLANES
SUB = 16
BAND = (WINDOW_CHUNKS + 1) * CHUNK
KV_HEAD = CHUNK + WINDOW_CHUNKS * CHUNK
PIECE = 512
GLU_PIECE = 256
VMEM_LIMIT_BYTES = 56 * 1024 * 1024

NT_DIMS = (((1,), (1,)), ((), ()))
TN_DIMS = (((0,), (0,)), ((), ()))


def _dot(a, b):
    return jnp.dot(a, b, preferred_element_type=F32)


def _sigmoid(x):
    return 0.5 * jnp.tanh(0.5 * x) + 0.5


def _silu(x):
    hx = 0.5 * x
    return hx * jnp.tanh(hx) + hx


def _gated(g, z):
    return z + jnp.tanh(0.5 * g) * z


def _hgrn_head(q_scr, k_scr, b_scr, v_ref, st_scr, r0, hd):
    ln = slice(hd * HG_D, (hd + 1) * HG_D)
    q = q_scr[r0:r0 + CHUNK, ln]
    k = k_scr[r0:r0 + CHUNK, ln]
    b = b_scr[r0:r0 + CHUNK, ln]
    v16 = v_ref[r0:r0 + CHUNK, 2 * D_HG + hd * HG_D:2 * D_HG + (hd + 1) * HG_D].astype(BF16)
    b_last = b[CHUNK - 1:CHUNK, :]
    st = st_scr[hd]
    q_dec = (q * jnp.exp2(b)).astype(BF16)
    o = lax.dot_general(q_dec, st.astype(BF16), NT_DIMS, preferred_element_type=F32)
    k_dec = (k * jnp.exp2(b_last - b)).astype(BF16)
    upd = lax.dot_general(v16, k_dec, TN_DIMS, preferred_element_type=F32)
    st_scr[hd] = st * jnp.exp2(b_last) + upd

    row = lax.broadcasted_iota(jnp.int32, (CHUNK, 1), 0)
    lane = lax.broadcasted_iota(jnp.int32, (SUBLANES, CHUNK), 1)
    srow = lax.broadcasted_iota(jnp.int32, (SUBLANES, CHUNK), 0)
    blocks = []
    for i in range(CHUNK // SUB):
        lo = i * SUB
        if i == 0:
            off = jnp.zeros((SUB, CHUNK), F32)
        else:
            b_ref = b[lo - 1:lo, :]
            qt = (q[lo:lo + SUB] * jnp.exp2(b[lo:lo + SUB] - b_ref)).astype(BF16)
            kt = jnp.where(row < lo, k * jnp.exp2(jnp.minimum(b_ref - b, 0.0)), 0.0).astype(BF16)
            off = lax.dot_general(qt, kt, NT_DIMS, preferred_element_type=F32)
        for half in range(SUB // SUBLANES):
            h0 = lo + half * SUBLANES
            qh = q[h0:h0 + SUBLANES]
            bh = b[h0:h0 + SUBLANES]
            sc = off[half * SUBLANES:(half + 1) * SUBLANES]
            for s in range((half + 1) * SUBLANES):
                ks = k_scr[r0 + lo + s:r0 + lo + s + 1, ln]
                bs = b_scr[r0 + lo + s:r0 + lo + s + 1, ln]
                col = jnp.sum(qh * ks * jnp.exp2(bh - bs), axis=-1, keepdims=True)
                keep = lane == lo + s
                if s >= half * SUBLANES:
                    keep = keep & (srow >= s - half * SUBLANES)
                sc = jnp.where(keep, col, sc)
            blocks.append(sc)
    scores = jnp.concatenate(blocks, axis=0).astype(BF16)
    return o + _dot(scores, v16)


def _layer_kernel(layer_ref, hm_ref, hx_ref, ng_ref, win_ref,
                  cw_ref, cb_ref, lng_ref, lnb_ref, wco_ref, lb_ref, hgn_ref, who_ref,
                  qn_ref, kn_ref, sinks_ref, wao_ref, wout_ref,
                  om_ref, ox_ref,
                  hn_scr, mix_scr, ext_scr, sh_scr, cwb_scr, ag_scr, za_scr, uh_scr, usw_scr, g_scr,
                  hq_scr, hk_scr, hb_scr, zb_scr, st_scr,
                  q_scr, kext_scr, vext_scr, oh_scr, zc_scr,
                  tri_scr, bdq_scr, bdk_scr, lbrow_scr):
    i = pl.program_id(0)
    layer = layer_ref[0]
    first_valid = jnp.where(i > 0, 0, FIRST_VALID)

    @pl.when(i == 0)
    def _init():
        ext_scr[0:CONV_PAD, :] = jnp.zeros((CONV_PAD, D_CONV), F32)
        st_scr[...] = jnp.zeros(st_scr.shape, F32)
        kext_scr[:, 0:KV_HEAD, :] = jnp.zeros((ATT_KV_HEADS, KV_HEAD, HEAD_DIM), BF16)
        vext_scr[:, 0:KV_HEAD, :] = jnp.zeros((ATT_KV_HEADS, KV_HEAD, HEAD_DIM), BF16)
        for j in range(CONV_WIDTH):
            cwb_scr[j] = jnp.broadcast_to(cw_ref[j:j + 1, :], (SUBLANES, D_CONV))
        r = lax.broadcasted_iota(jnp.int32, (CHUNK, CHUNK), 0)
        c = lax.broadcasted_iota(jnp.int32, (CHUNK, CHUNK), 1)
        tri_scr[...] = jnp.where(c <= r, 1.0, 0.0).astype(BF16)
        r = lax.broadcasted_iota(jnp.int32, (D_ATT, D_ATT), 0) // HEAD_DIM
        c = lax.broadcasted_iota(jnp.int32, (D_ATT, D_ATT), 1) // HEAD_DIM
        bdq_scr[...] = jnp.where(r == c, 1.0, 0.0).astype(BF16)
        r = lax.broadcasted_iota(jnp.int32, (D_KV, D_KV), 0) // HEAD_DIM
        c = lax.broadcasted_iota(jnp.int32, (D_KV, D_KV), 1) // HEAD_DIM
        bdk_scr[...] = jnp.where(r == c, 1.0, 0.0).astype(BF16)
        rows = [lb_ref[j:j + 1, :] for j in range(DEPTH)]
        mx = rows[0]
        for j in range(1, DEPTH):
            mx = jnp.maximum(mx, rows[j])
        ex = [jnp.exp(rw - mx) for rw in rows]
        tot = ex[0]
        for j in range(1, DEPTH):
            tot = tot + ex[j]
        sm = [e / tot for e in ex]
        cum = sm[0]
        lb_sel = jnp.zeros((1, D_HG), F32)
        for j in range(DEPTH):
            if j > 0:
                cum = cum + sm[j]
            lb_j = jnp.clip(cum - sm[0], 0.0, 1.0)
            lb_sel = jnp.where(layer == j, lb_j, lb_sel)
        lbrow_scr[...] = lb_sel

    def proj(dst_scr, w0, c0, n):
        def run():
            dst_scr[:, c0:c0 + n] = _dot(hn_scr[...], win_ref[:, w0 + c0:w0 + c0 + n])
        return run

    conv_side = ([proj(uh_scr, OFF_HG, c0, PIECE) for c0 in range(0, N_HG, PIECE)]
                 + [proj(usw_scr, OFF_SW, c0, min(PIECE, N_SW - c0)) for c0 in range(0, N_SW, PIECE)])
    hgrn_side = [proj(g_scr, OFF_G, c0, PIECE) for c0 in range(0, N_G, PIECE)]

    h = jnp.where(i == 0, hm_ref[...], hx_ref[...])
    ms = jnp.mean(h * h, axis=-1, keepdims=True)
    hn_scr[...] = (h * lax.rsqrt(ms + EPS) * ng_ref[...]).astype(BF16)

    row_t = lax.broadcasted_iota(jnp.int32, (TILE, 1), 0)
    for c0 in range(0, D_CONV, GLU_PIECE):
        u = (_dot(hn_scr[...], win_ref[:, c0:c0 + GLU_PIECE])
             * _sigmoid(_dot(hn_scr[...], win_ref[:, D_CONV + c0:D_CONV + c0 + GLU_PIECE])))
        ext_scr[CONV_PAD:CONV_PAD + TILE, c0:c0 + GLU_PIECE] = jnp.where(row_t >= first_valid, u, 0.0)
    ag_scr[...] = _dot(hn_scr[...], win_ref[:, 2 * D_CONV:N_A])
    for r in range(1, SUBLANES):
        sh_scr[r - 1] = ext_scr[r:r + CONV_SH_ROWS, :]
    for rb in range(TILE // CONV_RB):
        r0 = rb * CONV_RB
        acc = jnp.broadcast_to(cb_ref[...], (CONV_RB // SUBLANES, SUBLANES, D_CONV))
        for j in range(CONV_WIDTH):
            m = CONV_PAD - (CONV_WIDTH - 1) + j
            a0 = r0 + m - m % SUBLANES
            if m % SUBLANES == 0:
                src = ext_scr[a0:a0 + CONV_RB, :]
            else:
                src = sh_scr[m % SUBLANES - 1, a0:a0 + CONV_RB, :]
            acc = acc + cwb_scr[j][None] * src.reshape(CONV_RB // SUBLANES, SUBLANES, D_CONV)
        acc = acc.reshape(CONV_RB, D_CONV)
        mu = jnp.mean(acc, axis=-1, keepdims=True)
        d = acc - mu
        var = jnp.mean(d * d, axis=-1, keepdims=True)
        y = d * lax.rsqrt(var + EPS) * lng_ref[...] + lnb_ref[...]
        za_scr[r0:r0 + CONV_RB, :] = (_silu(y) * _silu(ag_scr[r0:r0 + CONV_RB, :])).astype(BF16)
        if rb < len(conv_side):
            conv_side[rb]()
    for run in conv_side[TILE // CONV_RB:]:
        run()
    ext_scr[0:CONV_PAD, :] = ext_scr[TILE:TILE + CONV_PAD, :]

    lb = lbrow_scr[...]
    tri = tri_scr[...]
    for c in range(NC):
        r0 = c * CHUNK
        rows = slice(r0, r0 + CHUNK)
        vrow = (lax.broadcasted_iota(jnp.int32, (CHUNK, 1), 0) + r0) >= first_valid
        zf = uh_scr[rows, D_HG:2 * D_HG]
        sg = jax.nn.sigmoid(zf)
        f = lb + (1.0 - lb) * sg
        logf = jnp.where(vrow, jnp.log(jnp.maximum(f, F_FLOOR)), 0.0)
        hk_scr[rows, :] = jnp.where(vrow, (1.0 - lb) * (1.0 - sg), 0.0)
        p0 = logf.astype(BF16)
        r1 = logf - p0.astype(F32)
        p1 = r1.astype(BF16)
        p2 = (r1 - p1.astype(F32)).astype(BF16)
        hb_scr[rows, :] = (_dot(tri, p0) + _dot(tri, p1) + _dot(tri, p2)) * LOG2E
        hq_scr[rows, :] = _silu(uh_scr[rows, 0:D_HG])
    side = list(hgrn_side)
    for c in range(NC):
        r0 = c * CHUNK
        rows = slice(r0, r0 + CHUNK)
        outs = []
        for hd in range(HG_HEADS):
            o = _hgrn_head(hq_scr, hk_scr, hb_scr, uh_scr, st_scr, r0, hd)
            o = o * lax.rsqrt(jnp.mean(o * o, axis=-1, keepdims=True) + EPS) * hgn_ref[...]
            outs.append(o)
            if side and hd % 2 == 1:
                side.pop(0)()
        o_all = jnp.concatenate(outs, axis=-1)
        zb_scr[rows, :] = (o_all * _silu(uh_scr[rows, 3 * D_HG:4 * D_HG])).astype(BF16)
    for run in side:
        run()
    mix_scr[...] = _gated(g_scr[:, 0:D_MODEL], _dot(za_scr[...], wco_ref[...]))

    q = usw_scr[:, 0:D_ATT]
    ssq = _dot((q * q).astype(BF16), bdq_scr[...])
    qn = (q * lax.rsqrt(ssq * (1.0 / HEAD_DIM) + EPS)
          * (qn_ref[...] * (HEAD_DIM ** -0.5 * LOG2E))).astype(BF16)
    for hq in range(ATT_Q_HEADS):
        q_scr[hq] = qn[:, hq * HEAD_DIM:(hq + 1) * HEAD_DIM]
    k = usw_scr[:, D_ATT:D_ATT + D_KV]
    ssk = _dot((k * k).astype(BF16), bdk_scr[...])
    kn = (k * lax.rsqrt(ssk * (1.0 / HEAD_DIM) + EPS) * kn_ref[...]).astype(BF16)
    vv = usw_scr[:, D_ATT + D_KV:D_ATT + 2 * D_KV].astype(BF16)
    for kv in range(ATT_KV_HEADS):
        kext_scr[kv, KV_HEAD:KV_HEAD + TILE, :] = kn[:, kv * HEAD_DIM:(kv + 1) * HEAD_DIM]
        vext_scr[kv, KV_HEAD:KV_HEAD + TILE, :] = vv[:, kv * HEAD_DIM:(kv + 1) * HEAD_DIM]

    grow = lax.broadcasted_iota(jnp.int32, (ATT_GROUP * CHUNK, 1), 0)
    sink_cols = []
    for kv in range(ATT_KV_HEADS):
        sc = jnp.zeros((ATT_GROUP * CHUNK, 1), F32)
        for j in range(ATT_GROUP):
            sc = jnp.where(grow // CHUNK == j, sinks_ref[layer, kv * ATT_GROUP + j] * LOG2E, sc)
        sink_cols.append(sc)
    col = lax.broadcasted_iota(jnp.int32, (1, BAND + CHUNK), 1)
    for c in range(NC):
        r0 = c * CHUNK
        cg = i * NC + c
        band_lo = FIRST_VALID - (cg - WINDOW_CHUNKS) * CHUNK
        meta_lo = jnp.where(cg > NC - 1 + WINDOW_CHUNKS, BAND + CHUNK - N_META, BAND + CHUNK)
        ok = ((col >= band_lo) & (col < BAND)) | (col >= meta_lo)
        bias = jnp.where(ok, 0.0, -jnp.inf)
        for kv in range(ATT_KV_HEADS):
            qg = jnp.concatenate([q_scr[kv * ATT_GROUP + j, r0:r0 + CHUNK, :]
                                  for j in range(ATT_GROUP)], axis=0)
            kb = jnp.concatenate([kext_scr[kv, r0 + CHUNK:r0 + CHUNK + BAND, :],
                                  kext_scr[kv, 0:CHUNK, :]], axis=0)
            vb = jnp.concatenate([vext_scr[kv, r0 + CHUNK:r0 + CHUNK + BAND, :],
                                  vext_scr[kv, 0:CHUNK, :]], axis=0)
            s = lax.dot_general(qg, kb, NT_DIMS, preferred_element_type=F32) + bias
            sink = sink_cols[kv]
            m = jnp.maximum(jnp.max(s, axis=-1, keepdims=True), sink)
            p = jnp.exp2(s - m)
            den = jnp.sum(p, axis=-1, keepdims=True) + jnp.exp2(sink - m)
            o = _dot(p.astype(BF16), vb) / den
            for j in range(ATT_GROUP):
                oh_scr[kv * ATT_GROUP + j, r0:r0 + CHUNK, :] = o[j * CHUNK:(j + 1) * CHUNK, :]
        if c == 0:
            mix_scr[...] += _gated(g_scr[:, D_MODEL:2 * D_MODEL], _dot(zb_scr[...], who_ref[...]))
    o_all = jnp.concatenate([oh_scr[hq] for hq in range(ATT_Q_HEADS)], axis=-1)
    zc_scr[...] = (o_all * _silu(usw_scr[:, D_ATT + 2 * D_KV:N_SW])).astype(BF16)

    for kv in range(ATT_KV_HEADS):
        for ext in (kext_scr, vext_scr):
            last = ext[kv, KV_HEAD + TILE - CHUNK:KV_HEAD + TILE, :]
            ext[kv, 0:CHUNK, :] = jnp.where(i == 0, last, ext[kv, 0:CHUNK, :])
            ext[kv, CHUNK:KV_HEAD, :] = ext[kv, TILE + CHUNK:TILE + KV_HEAD, :]

    mixed = 0.5 * (mix_scr[...] + _gated(g_scr[:, 2 * D_MODEL:3 * D_MODEL], _dot(zc_scr[...], wao_ref[...])))
    delta = _dot(mixed.astype(BF16), wout_ref[...])

    @pl.when(i == 0)
    def _store_meta():
        om_ref[...] = hm_ref[...] + delta
        ox_ref[...] = hx_ref[...]

    @pl.when(i > 0)
    def _store():
        ox_ref[...] = hx_ref[...] + delta


def _layer_call():
    def per_layer(shape):
        return pl.BlockSpec((None,) + shape, lambda i, l: (l[0],) + (0,) * len(shape),
                            pipeline_mode=pl.Buffered(1))

    def whole(shape):
        return pl.BlockSpec(shape, lambda i, l: (0,) * len(shape), pipeline_mode=pl.Buffered(1))

    x_spec = pl.BlockSpec((TILE, D_MODEL), lambda i, l: (jnp.maximum(i - 1, 0), 0))
    m_spec = pl.BlockSpec((TILE, D_MODEL), lambda i, l: (0, 0))
    in_specs = [
        m_spec,
        x_spec,
        per_layer((1, D_MODEL)),
        per_layer((D_MODEL, D_IN)),
        per_layer((CONV_WIDTH, D_CONV)),
        per_layer((1, D_CONV)),
        per_layer((1, D_CONV)),
        per_layer((1, D_CONV)),
        per_layer((D_CONV, D_MODEL)),
        whole((DEPTH, D_HG)),
        per_layer((1, HG_D)),
        per_layer((D_HG, D_MODEL)),
        per_layer((1, D_ATT)),
        per_layer((1, D_KV)),
        pl.BlockSpec(memory_space=pltpu.SMEM),
        per_layer((D_ATT, D_MODEL)),
        per_layer((D_MODEL, D_MODEL)),
    ]
    scratch = [
        pltpu.VMEM((TILE, D_MODEL), BF16),
        pltpu.VMEM((TILE, D_MODEL), F32),
        pltpu.VMEM((TILE + CONV_PAD, D_CONV), F32),
        pltpu.VMEM((SUBLANES - 1, CONV_SH_ROWS, D_CONV), F32),
        pltpu.VMEM((CONV_WIDTH, SUBLANES, D_CONV), F32),
        pltpu.VMEM((TILE, D_CONV), F32),
        pltpu.VMEM((TILE, D_CONV), BF16),
        pltpu.VMEM((TILE, N_HG), F32),
        pltpu.VMEM((TILE, N_SW), F32),
        pltpu.VMEM((TILE, N_G), F32),
        pltpu.VMEM((TILE, D_HG), F32),
        pltpu.VMEM((TILE, D_HG), F32),
        pltpu.VMEM((TILE, D_HG), F32),
        pltpu.VMEM((TILE, D_HG), BF16),
        pltpu.VMEM((HG_HEADS, HG_D, HG_D), F32),
        pltpu.VMEM((ATT_Q_HEADS, TILE, HEAD_DIM), BF16),
        pltpu.VMEM((ATT_KV_HEADS, KV_HEAD + TILE, HEAD_DIM), BF16),
        pltpu.VMEM((ATT_KV_HEADS, KV_HEAD + TILE, HEAD_DIM), BF16),
        pltpu.VMEM((ATT_Q_HEADS, TILE, HEAD_DIM), F32),
        pltpu.VMEM((TILE, D_ATT), BF16),
        pltpu.VMEM((CHUNK, CHUNK), BF16),
        pltpu.VMEM((D_ATT, D_ATT), BF16),
        pltpu.VMEM((D_KV, D_KV), BF16),
        pltpu.VMEM((1, D_HG), F32),
    ]
    return pl.pallas_call(
        _layer_kernel,
        grid_spec=pltpu.PrefetchScalarGridSpec(
            num_scalar_prefetch=1,
            grid=(N_TILES + 1,),
            in_specs=in_specs,
            out_specs=[m_spec, x_spec],
            scratch_shapes=scratch,
        ),
        out_shape=[jax.ShapeDtypeStruct((TILE, D_MODEL), F32),
                   jax.ShapeDtypeStruct((SEQ, D_MODEL), F32)],
        compiler_params=pltpu.CompilerParams(
            dimension_semantics=("arbitrary",),
            vmem_limit_bytes=VMEM_LIMIT_BYTES,
        ),
        name="trunk_layer",
    )


def kernel(x, meta_tokens, norm_g, w_in, conv_w, conv_b, conv_ln_g, conv_ln_b, w_conv_out,
           hg_lower_bounds, hg_norm_g, w_hg_out, q_norm_g, k_norm_g, attn_sinks, w_att_out, w_out):
    assert x.shape == (1, SEQ, D_MODEL) and w_in.shape == (DEPTH, D_MODEL, D_IN)
    hx = x[0].astype(F32)
    hm = jnp.concatenate([jnp.zeros((FIRST_VALID, D_MODEL), F32), meta_tokens.astype(F32)], axis=0)
    params = (
        norm_g.astype(F32)[:, None, :], w_in.astype(BF16),
        conv_w.astype(F32), conv_b.astype(F32)[:, None, :],
        conv_ln_g.astype(F32)[:, None, :], conv_ln_b.astype(F32)[:, None, :],
        w_conv_out.astype(BF16),
        hg_lower_bounds.astype(F32), hg_norm_g.astype(F32)[:, None, :], w_hg_out.astype(BF16),
        jnp.tile(q_norm_g.astype(F32), (1, ATT_Q_HEADS))[:, None, :],
        jnp.tile(k_norm_g.astype(F32), (1, ATT_KV_HEADS))[:, None, :],
        attn_sinks.astype(F32), w_att_out.astype(BF16), w_out.astype(BF16),
    )
    call = _layer_call()
    for layer in range(DEPTH):
        hm, hx = call(jnp.full((1,), layer, jnp.int32), hm, hx, *params)
    return hx[None].astype(x.dtype)
```

```python
import math

import jax
import jax.numpy as jnp
from jax import lax
from jax.experimental import pallas as pl
from jax.experimental.pallas import tpu as pltpu

F32 = jnp.float32
BF16 = jnp.bfloat16

D_MODEL = 1024
SEQ = 16384
DEPTH = 4
CHUNK = 64
N_META = 16
D_CONV = 512
CONV_WIDTH = 31
HG_HEADS = 4
HG_D = 128
D_HG = HG_HEADS * HG_D
F_FLOOR = 1e-30
ATT_Q_HEADS = 8
ATT_KV_HEADS = 2
HEAD_DIM = 64
ATT_GROUP = ATT_Q_HEADS // ATT_KV_HEADS
D_ATT = ATT_Q_HEADS * HEAD_DIM
D_KV = ATT_KV_HEADS * HEAD_DIM
WINDOW_CHUNKS = 2
EPS = 1e-6
LOG2E = math.log2(math.e)

N_A = 2 * D_CONV + D_CONV
N_HG = 4 * D_HG
N_SW = 2 * D_ATT + 2 * D_KV
N_G = 3 * D_MODEL
OFF_HG = N_A
OFF_SW = OFF_HG + N_HG
OFF_G = OFF_SW + N_SW
D_IN = OFF_G + N_G

TILE = 256
NC = TILE // CHUNK
N_TILES = SEQ // TILE
TILES_PER_STEP = 2
META_SUB = TILES_PER_STEP - 1
FIRST_VALID = TILE - N_META
FIRST_VALID_POS = META_SUB * TILE + FIRST_VALID
META_CHUNK = TILES_PER_STEP * NC - 1
SUBLANES = 8
CONV_PAD = 32
CONV_RB = 32
CONV_SH_ROWS = TILE + CONV_PAD - SUBLANES
SUB = 16
BAND = (WINDOW_CHUNKS + 1) * CHUNK
KV_HEAD = CHUNK + WINDOW_CHUNKS * CHUNK
PIECE = 512
GLU_PIECE = 256
VMEM_LIMIT_BYTES = 56 * 1024 * 1024

NT_DIMS = (((1,), (1,)), ((), ()))
TN_DIMS = (((0,), (0,)), ((), ()))


def _dot(a, b):
    return jnp.dot(a, b, preferred_element_type=F32)


def _sigmoid(x):
    return 0.5 * jnp.tanh(0.5 * x) + 0.5


def _silu(x):
    hx = 0.5 * x
    return hx * jnp.tanh(hx) + hx


def _gated(g, z):
    return z + jnp.tanh(0.5 * g) * z


def _hgrn_head(q_scr, k_scr, b_scr, v_ref, st_scr, r0, hd):
    ln = slice(hd * HG_D, (hd + 1) * HG_D)
    q = q_scr[r0:r0 + CHUNK, ln]
    k = k_scr[r0:r0 + CHUNK, ln]
    b = b_scr[r0:r0 + CHUNK, ln]
    v16 = v_ref[r0:r0 + CHUNK, 2 * D_HG + hd * HG_D:2 * D_HG + (hd + 1) * HG_D].astype(BF16)
    b_last = b[CHUNK - 1:CHUNK, :]
    st = st_scr[hd]
    q_dec = (q * jnp.exp2(b)).astype(BF16)
    o = lax.dot_general(q_dec, st.astype(BF16), NT_DIMS, preferred_element_type=F32)
    k_dec = (k * jnp.exp2(b_last - b)).astype(BF16)
    upd = lax.dot_general(v16, k_dec, TN_DIMS, preferred_element_type=F32)
    st_scr[hd] = st * jnp.exp2(b_last) + upd

    row = lax.broadcasted_iota(jnp.int32, (CHUNK, 1), 0)
    lane = lax.broadcasted_iota(jnp.int32, (SUBLANES, CHUNK), 1)
    srow = lax.broadcasted_iota(jnp.int32, (SUBLANES, CHUNK), 0)
    blocks = []
    for i in range(CHUNK // SUB):
        lo = i * SUB
        if i == 0:
            off = jnp.zeros((SUB, CHUNK), F32)
        else:
            b_ref = b[lo - 1:lo, :]
            qt = (q[lo:lo + SUB] * jnp.exp2(b[lo:lo + SUB] - b_ref)).astype(BF16)
            kt = jnp.where(row < lo, k * jnp.exp2(jnp.minimum(b_ref - b, 0.0)), 0.0).astype(BF16)
            off = lax.dot_general(qt, kt, NT_DIMS, preferred_element_type=F32)
        for half in range(SUB // SUBLANES):
            h0 = lo + half * SUBLANES
            qh = q[h0:h0 + SUBLANES]
            bh = b[h0:h0 + SUBLANES]
            sc = off[half * SUBLANES:(half + 1) * SUBLANES]
            for s in range((half + 1) * SUBLANES):
                ks = k_scr[r0 + lo + s:r0 + lo + s + 1, ln]
                bs = b_scr[r0 + lo + s:r0 + lo + s + 1, ln]
                col = jnp.sum(qh * ks * jnp.exp2(bh - bs), axis=-1, keepdims=True)
                keep = lane == lo + s
                if s >= half * SUBLANES:
                    keep = keep & (srow >= s - half * SUBLANES)
                sc = jnp.where(keep, col, sc)
            blocks.append(sc)
    scores = jnp.concatenate(blocks, axis=0).astype(BF16)
    return o + _dot(scores, v16)


def _layer_kernel(*refs):
    for sub in range(TILES_PER_STEP):
        _tile(sub, *refs)


def _tile(sub, layer_ref, hm_ref, hx_ref, ng_ref, win_ref,
          cw_ref, cb_ref, lng_ref, lnb_ref, wco_ref, lb_ref, hgn_ref, who_ref,
          qn_ref, kn_ref, sinks_ref, wao_ref, wout_ref,
          om_ref, ox_ref,
          hn_scr, mix_scr, ext_scr, sh_scr, cwb_scr, ag_scr, za_scr, uh_scr, usw_scr, g_scr,
          hq_scr, hk_scr, hb_scr, zb_scr, st_scr,
          q_scr, kext_scr, vext_scr, oh_scr, zc_scr,
          tri_scr, bdq_scr, bdk_scr, lbrow_scr):
    i = pl.program_id(0)
    layer = layer_ref[0]
    is_meta = sub == META_SUB
    first_valid = jnp.where(i > 0, 0, FIRST_VALID if is_meta else TILE)
    tile_rows = slice(sub * TILE, (sub + 1) * TILE)

    def load_h():
        return jnp.where(i == 0, hm_ref[...] if is_meta else 0.0, hx_ref[tile_rows, :])

    once = pl.when(i == 0) if sub == 0 else (lambda body: None)

    @once
    def _init():
        ext_scr[0:CONV_PAD, :] = jnp.zeros((CONV_PAD, D_CONV), F32)
        st_scr[...] = jnp.zeros(st_scr.shape, F32)
        kext_scr[:, 0:KV_HEAD, :] = jnp.zeros((ATT_KV_HEADS, KV_HEAD, HEAD_DIM), BF16)
        vext_scr[:, 0:KV_HEAD, :] = jnp.zeros((ATT_KV_HEADS, KV_HEAD, HEAD_DIM), BF16)
        for j in range(CONV_WIDTH):
            cwb_scr[j] = jnp.broadcast_to(cw_ref[j:j + 1, :], (SUBLANES, D_CONV))
        r = lax.broadcasted_iota(jnp.int32, (CHUNK, CHUNK), 0)
        c = lax.broadcasted_iota(jnp.int32, (CHUNK, CHUNK), 1)
        tri_scr[...] = jnp.where(c <= r, 1.0, 0.0).astype(BF16)
        r = lax.broadcasted_iota(jnp.int32, (D_ATT, D_ATT), 0) // HEAD_DIM
        c = lax.broadcasted_iota(jnp.int32, (D_ATT, D_ATT), 1) // HEAD_DIM
        bdq_scr[...] = jnp.where(r == c, 1.0, 0.0).astype(BF16)
        r = lax.broadcasted_iota(jnp.int32, (D_KV, D_KV), 0) // HEAD_DIM
        c = lax.broadcasted_iota(jnp.int32, (D_KV, D_KV), 1) // HEAD_DIM
        bdk_scr[...] = jnp.where(r == c, 1.0, 0.0).astype(BF16)
        rows = [lb_ref[j:j + 1, :] for j in range(DEPTH)]
        mx = rows[0]
        for j in range(1, DEPTH):
            mx = jnp.maximum(mx, rows[j])
        ex = [jnp.exp(rw - mx) for rw in rows]
        tot = ex[0]
        for j in range(1, DEPTH):
            tot = tot + ex[j]
        sm = [e / tot for e in ex]
        cum = sm[0]
        lb_sel = jnp.zeros((1, D_HG), F32)
        for j in range(DEPTH):
            if j > 0:
                cum = cum + sm[j]
            lb_j = jnp.clip(cum - sm[0], 0.0, 1.0)
            lb_sel = jnp.where(layer == j, lb_j, lb_sel)
        lbrow_scr[...] = lb_sel

    def proj(dst_scr, w0, c0, n):
        def run():
            dst_scr[:, c0:c0 + n] = _dot(hn_scr[...], win_ref[:, w0 + c0:w0 + c0 + n])
        return run

    conv_side = ([proj(uh_scr, OFF_HG, c0, PIECE) for c0 in range(0, N_HG, PIECE)]
                 + [proj(usw_scr, OFF_SW, c0, min(PIECE, N_SW - c0)) for c0 in range(0, N_SW, PIECE)])
    hgrn_side = [proj(g_scr, OFF_G, c0, PIECE) for c0 in range(0, N_G, PIECE)]

    h = load_h()
    ms = jnp.mean(h * h, axis=-1, keepdims=True)
    hn_scr[...] = (h * lax.rsqrt(ms + EPS) * ng_ref[...]).astype(BF16)

    row_t = lax.broadcasted_iota(jnp.int32, (TILE, 1), 0)
    for c0 in range(0, D_CONV, GLU_PIECE):
        u = (_dot(hn_scr[...], win_ref[:, c0:c0 + GLU_PIECE])
             * _sigmoid(_dot(hn_scr[...], win_ref[:, D_CONV + c0:D_CONV + c0 + GLU_PIECE])))
        ext_scr[CONV_PAD:CONV_PAD + TILE, c0:c0 + GLU_PIECE] = jnp.where(row_t >= first_valid, u, 0.0)
    ag_scr[...] = _dot(hn_scr[...], win_ref[:, 2 * D_CONV:N_A])
    for r in range(1, SUBLANES):
        sh_scr[r - 1] = ext_scr[r:r + CONV_SH_ROWS, :]
    for rb in range(TILE // CONV_RB):
        r0 = rb * CONV_RB
        acc = jnp.broadcast_to(cb_ref[...], (CONV_RB // SUBLANES, SUBLANES, D_CONV))
        for j in range(CONV_WIDTH):
            m = CONV_PAD - (CONV_WIDTH - 1) + j
            a0 = r0 + m - m % SUBLANES
            if m % SUBLANES == 0:
                src = ext_scr[a0:a0 + CONV_RB, :]
            else:
                src = sh_scr[m % SUBLANES - 1, a0:a0 + CONV_RB, :]
            acc = acc + cwb_scr[j][None] * src.reshape(CONV_RB // SUBLANES, SUBLANES, D_CONV)
        acc = acc.reshape(CONV_RB, D_CONV)
        mu = jnp.mean(acc, axis=-1, keepdims=True)
        d = acc - mu
        var = jnp.mean(d * d, axis=-1, keepdims=True)
        y = d * lax.rsqrt(var + EPS) * lng_ref[...] + lnb_ref[...]
        za_scr[r0:r0 + CONV_RB, :] = (_silu(y) * _silu(ag_scr[r0:r0 + CONV_RB, :])).astype(BF16)
        if rb < len(conv_side):
            conv_side[rb]()
    for run in conv_side[TILE // CONV_RB:]:
        run()
    ext_scr[0:CONV_PAD, :] = ext_scr[TILE:TILE + CONV_PAD, :]

    lb = lbrow_scr[...]
    tri = tri_scr[...]
    for c in range(NC):
        r0 = c * CHUNK
        rows = slice(r0, r0 + CHUNK)
        vrow = (lax.broadcasted_iota(jnp.int32, (CHUNK, 1), 0) + r0) >= first_valid
        zf = uh_scr[rows, D_HG:2 * D_HG]
        sg = jax.nn.sigmoid(zf)
        f = lb + (1.0 - lb) * sg
        logf = jnp.where(vrow, jnp.log(jnp.maximum(f, F_FLOOR)), 0.0)
        hk_scr[rows, :] = jnp.where(vrow, (1.0 - lb) * (1.0 - sg), 0.0)
        p0 = logf.astype(BF16)
        r1 = logf - p0.astype(F32)
        p1 = r1.astype(BF16)
        p2 = (r1 - p1.astype(F32)).astype(BF16)
        hb_scr[rows, :] = (_dot(tri, p0) + _dot(tri, p1) + _dot(tri, p2)) * LOG2E
        hq_scr[rows, :] = _silu(uh_scr[rows, 0:D_HG])
    side = list(hgrn_side)
    for c in range(NC):
        r0 = c * CHUNK
        rows = slice(r0, r0 + CHUNK)
        outs = []
        for hd in range(HG_HEADS):
            o = _hgrn_head(hq_scr, hk_scr, hb_scr, uh_scr, st_scr, r0, hd)
            o = o * lax.rsqrt(jnp.mean(o * o, axis=-1, keepdims=True) + EPS) * hgn_ref[...]
            outs.append(o)
            if side and hd % 2 == 1:
                side.pop(0)()
        o_all = jnp.concatenate(outs, axis=-1)
        zb_scr[rows, :] = (o_all * _silu(uh_scr[rows, 3 * D_HG:4 * D_HG])).astype(BF16)
    for run in side:
        run()
    mix_scr[...] = _gated(g_scr[:, 0:D_MODEL], _dot(za_scr[...], wco_ref[...]))

    q = usw_scr[:, 0:D_ATT]
    ssq = _dot((q * q).astype(BF16), bdq_scr[...])
    qn = (q * lax.rsqrt(ssq * (1.0 / HEAD_DIM) + EPS)
          * (qn_ref[...] * (HEAD_DIM ** -0.5 * LOG2E))).astype(BF16)
    for hq in range(ATT_Q_HEADS):
        q_scr[hq] = qn[:, hq * HEAD_DIM:(hq + 1) * HEAD_DIM]
    k = usw_scr[:, D_ATT:D_ATT + D_KV]
    ssk = _dot((k * k).astype(BF16), bdk_scr[...])
    kn = (k * lax.rsqrt(ssk * (1.0 / HEAD_DIM) + EPS) * kn_ref[...]).astype(BF16)
    vv = usw_scr[:, D_ATT + D_KV:D_ATT + 2 * D_KV].astype(BF16)
    for kv in range(ATT_KV_HEADS):
        kext_scr[kv, KV_HEAD:KV_HEAD + TILE, :] = kn[:, kv * HEAD_DIM:(kv + 1) * HEAD_DIM]
        vext_scr[kv, KV_HEAD:KV_HEAD + TILE, :] = vv[:, kv * HEAD_DIM:(kv + 1) * HEAD_DIM]

    grow = lax.broadcasted_iota(jnp.int32, (ATT_GROUP * CHUNK, 1), 0)
    sink_cols = []
    for kv in range(ATT_KV_HEADS):
        sc = jnp.zeros((ATT_GROUP * CHUNK, 1), F32)
        for j in range(ATT_GROUP):
            sc = jnp.where(grow // CHUNK == j, sinks_ref[layer, kv * ATT_GROUP + j] * LOG2E, sc)
        sink_cols.append(sc)
    col = lax.broadcasted_iota(jnp.int32, (1, BAND + CHUNK), 1)
    for c in range(NC):
        r0 = c * CHUNK
        cg = (i * TILES_PER_STEP + sub) * NC + c
        band_lo = FIRST_VALID_POS - (cg - WINDOW_CHUNKS) * CHUNK
        meta_lo = jnp.where(cg > META_CHUNK + WINDOW_CHUNKS, BAND + CHUNK - N_META, BAND + CHUNK)
        ok = ((col >= band_lo) & (col < BAND)) | (col >= meta_lo)
        bias = jnp.where(ok, 0.0, -jnp.inf)
        for kv in range(ATT_KV_HEADS):
            qg = jnp.concatenate([q_scr[kv * ATT_GROUP + j, r0:r0 + CHUNK, :]
                                  for j in range(ATT_GROUP)], axis=0)
            kb = jnp.concatenate([kext_scr[kv, r0 + CHUNK:r0 + CHUNK + BAND, :],
                                  kext_scr[kv, 0:CHUNK, :]], axis=0)
            vb = jnp.concatenate([vext_scr[kv, r0 + CHUNK:r0 + CHUNK + BAND, :],
                                  vext_scr[kv, 0:CHUNK, :]], axis=0)
            s = lax.dot_general(qg, kb, NT_DIMS, preferred_element_type=F32) + bias
            sink = sink_cols[kv]
            m = jnp.maximum(jnp.max(s, axis=-1, keepdims=True), sink)
            p = jnp.exp2(s - m)
            den = jnp.sum(p, axis=-1, keepdims=True) + jnp.exp2(sink - m)
            o = _dot(p.astype(BF16), vb) / den
            for j in range(ATT_GROUP):
                oh_scr[kv * ATT_GROUP + j, r0:r0 + CHUNK, :] = o[j * CHUNK:(j + 1) * CHUNK, :]
        if c == 0:
            mix_scr[...] += _gated(g_scr[:, D_MODEL:2 * D_MODEL], _dot(zb_scr[...], who_ref[...]))
    o_all = jnp.concatenate([oh_scr[hq] for hq in range(ATT_Q_HEADS)], axis=-1)
    zc_scr[...] = (o_all * _silu(usw_scr[:, D_ATT + 2 * D_KV:N_SW])).astype(BF16)

    for kv in range(ATT_KV_HEADS):
        for ext in (kext_scr, vext_scr):
            last = ext[kv, KV_HEAD + TILE - CHUNK:KV_HEAD + TILE, :]
            if is_meta:
                ext[kv, 0:CHUNK, :] = jnp.where(i == 0, last, ext[kv, 0:CHUNK, :])
            ext[kv, CHUNK:KV_HEAD, :] = ext[kv, TILE + CHUNK:TILE + KV_HEAD, :]

    mixed = 0.5 * (mix_scr[...] + _gated(g_scr[:, 2 * D_MODEL:3 * D_MODEL], _dot(zc_scr[...], wao_ref[...])))
    delta = _dot(mixed.astype(BF16), wout_ref[...])

    ox_ref[tile_rows, :] = load_h() + delta
    if is_meta:
        @pl.when(i == 0)
        def _store_meta():
            om_ref[...] = ox_ref[tile_rows, :]


def _layer_call():
    def per_layer(shape):
        return pl.BlockSpec((None,) + shape, lambda i, l: (l[0],) + (0,) * len(shape),
                            pipeline_mode=pl.Buffered(1))

    def whole(shape):
        return pl.BlockSpec(shape, lambda i, l: (0,) * len(shape), pipeline_mode=pl.Buffered(1))

    x_spec = pl.BlockSpec((TILES_PER_STEP * TILE, D_MODEL), lambda i, l: (jnp.maximum(i - 1, 0), 0))
    m_spec = pl.BlockSpec((TILE, D_MODEL), lambda i, l: (0, 0))
    in_specs = [
        m_spec,
        x_spec,
        per_layer((1, D_MODEL)),
        per_layer((D_MODEL, D_IN)),
        per_layer((CONV_WIDTH, D_CONV)),
        per_layer((1, D_CONV)),
        per_layer((1, D_CONV)),
        per_layer((1, D_CONV)),
        per_layer((D_CONV, D_MODEL)),
        whole((DEPTH, D_HG)),
        per_layer((1, HG_D)),
        per_layer((D_HG, D_MODEL)),
        per_layer((1, D_ATT)),
        per_layer((1, D_KV)),
        pl.BlockSpec(memory_space=pltpu.SMEM),
        per_layer((D_ATT, D_MODEL)),
        per_layer((D_MODEL, D_MODEL)),
    ]
    scratch = [
        pltpu.VMEM((TILE, D_MODEL), BF16),
        pltpu.VMEM((TILE, D_MODEL), F32),
        pltpu.VMEM((TILE + CONV_PAD, D_CONV), F32),
        pltpu.VMEM((SUBLANES - 1, CONV_SH_ROWS, D_CONV), F32),
        pltpu.VMEM((CONV_WIDTH, SUBLANES, D_CONV), F32),
        pltpu.VMEM((TILE, D_CONV), F32),
        pltpu.VMEM((TILE, D_CONV), BF16),
        pltpu.VMEM((TILE, N_HG), F32),
        pltpu.VMEM((TILE, N_SW), F32),
        pltpu.VMEM((TILE, N_G), F32),
        pltpu.VMEM((TILE, D_HG), F32),
        pltpu.VMEM((TILE, D_HG), F32),
        pltpu.VMEM((TILE, D_HG), F32),
        pltpu.VMEM((TILE, D_HG), BF16),
        pltpu.VMEM((HG_HEADS, HG_D, HG_D), F32),
        pltpu.VMEM((ATT_Q_HEADS, TILE, HEAD_DIM), BF16),
        pltpu.VMEM((ATT_KV_HEADS, KV_HEAD + TILE, HEAD_DIM), BF16),
        pltpu.VMEM((ATT_KV_HEADS, KV_HEAD + TILE, HEAD_DIM), BF16),
        pltpu.VMEM((ATT_Q_HEADS, TILE, HEAD_DIM), F32),
        pltpu.VMEM((TILE, D_ATT), BF16),
        pltpu.VMEM((CHUNK, CHUNK), BF16),
        pltpu.VMEM((D_ATT, D_ATT), BF16),
        pltpu.VMEM((D_KV, D_KV), BF16),
        pltpu.VMEM((1, D_HG), F32),
    ]
    return pl.pallas_call(
        _layer_kernel,
        grid_spec=pltpu.PrefetchScalarGridSpec(
            num_scalar_prefetch=1,
            grid=(N_TILES // TILES_PER_STEP + 1,),
            in_specs=in_specs,
            out_specs=[m_spec, x_spec],
            scratch_shapes=scratch,
        ),
        out_shape=[jax.ShapeDtypeStruct((TILE, D_MODEL), F32),
                   jax.ShapeDtypeStruct((SEQ, D_MODEL), F32)],
        compiler_params=pltpu.CompilerParams(
            dimension_semantics=("arbitrary",),
            vmem_limit_bytes=VMEM_LIMIT_BYTES,
        ),
        name="trunk_layer",
    )


def kernel(x, meta_tokens, norm_g, w_in, conv_w, conv_b, conv_ln_g, conv_ln_b, w_conv_out,
           hg_lower_bounds, hg_norm_g, w_hg_out, q_norm_g, k_norm_g, attn_sinks, w_att_out, w_out):
    assert x.shape == (1, SEQ, D_MODEL) and w_in.shape == (DEPTH, D_MODEL, D_IN)
    hx = x[0].astype(F32)
    hm = jnp.concatenate([jnp.zeros((FIRST_VALID, D_MODEL), F32), meta_tokens.astype(F32)], axis=0)
    params = (
        norm_g.astype(F32)[:, None, :], w_in.astype(BF16),
        conv_w.astype(F32), conv_b.astype(F32)[:, None, :],
        conv_ln_g.astype(F32)[:, None, :], conv_ln_b.astype(F32)[:, None, :],
        w_conv_out.astype(BF16),
        hg_lower_bounds.astype(F32), hg_norm_g.astype(F32)[:, None, :], w_hg_out.astype(BF16),
        jnp.tile(q_norm_g.astype(F32), (1, ATT_Q_HEADS))[:, None, :],
        jnp.tile(k_norm_g.astype(F32), (1, ATT_KV_HEADS))[:, None, :],
        attn_sinks.astype(F32), w_att_out.astype(BF16), w_out.astype(BF16),
    )
    call = _layer_call()
    for layer in range(DEPTH):
        hm, hx = call(jnp.full((1,), layer, jnp.int32), hm, hx, *params)
    return hx[None].astype(x.dtype)
```

```python
import math

import jax
import jax.numpy as jnp
from jax import lax
from jax.experimental import pallas as pl
from jax.experimental.pallas import tpu as pltpu

F32 = jnp.float32
BF16 = jnp.bfloat16

D_MODEL = 1024
SEQ = 16384
DEPTH = 4
CHUNK = 64
N_META = 16
D_CONV = 512
CONV_WIDTH = 31
HG_HEADS = 4
HG_D = 128
D_HG = HG_HEADS * HG_D
F_FLOOR = 1e-30
ATT_Q_HEADS = 8
ATT_KV_HEADS = 2
HEAD_DIM = 64
ATT_GROUP = ATT_Q_HEADS // ATT_KV_HEADS
D_ATT = ATT_Q_HEADS * HEAD_DIM
D_KV = ATT_KV_HEADS * HEAD_DIM
WINDOW_CHUNKS = 2
EPS = 1e-6
LOG2E = math.log2(math.e)

N_A = 2 * D_CONV + D_CONV
N_HG = 4 * D_HG
N_SW = 2 * D_ATT + 2 * D_KV
N_G = 3 * D_MODEL
OFF_HG = N_A
OFF_SW = OFF_HG + N_HG
OFF_G = OFF_SW + N_SW
D_IN = OFF_G + N_G

TILE = 256
NC = TILE // CHUNK
N_TILES = SEQ // TILE
FIRST_VALID = TILE - N_META
SUBLANES = 8
CONV_PAD = 32
CONV_RB = 32
CONV_SH_ROWS = TILE + CONV_PAD - SUBLANES
SUB = 16
BAND = (WINDOW_CHUNKS + 1) * CHUNK
KV_HEAD = CHUNK + WINDOW_CHUNKS * CHUNK
PIECE = 512
GLU_PIECE = 256
VMEM_LIMIT_BYTES = 56 * 1024 * 1024

NT_DIMS = (((1,), (1,)), ((), ()))
TN_DIMS = (((0,), (0,)), ((), ()))


def _dot(a, b):
    return jnp.dot(a, b, preferred_element_type=F32)


def _sigmoid(x):
    return 0.5 * jnp.tanh(0.5 * x) + 0.5


def _silu(x):
    hx = 0.5 * x
    return hx * jnp.tanh(hx) + hx


def _gated(g, z):
    return z + jnp.tanh(0.5 * g) * z


def _hgrn_head(q_scr, k_scr, b_scr, v_ref, st_scr, r0, hd):
    ln = slice(hd * HG_D, (hd + 1) * HG_D)
    q = q_scr[r0:r0 + CHUNK, ln]
    k = k_scr[r0:r0 + CHUNK, ln]
    b = b_scr[r0:r0 + CHUNK, ln]
    v16 = v_ref[r0:r0 + CHUNK, 2 * D_HG + hd * HG_D:2 * D_HG + (hd + 1) * HG_D].astype(BF16)
    b_last = b[CHUNK - 1:CHUNK, :]
    st = st_scr[hd]
    q_dec = (q * jnp.exp2(b)).astype(BF16)
    o = lax.dot_general(q_dec, st.astype(BF16), NT_DIMS, preferred_element_type=F32)
    k_dec = (k * jnp.exp2(b_last - b)).astype(BF16)
    upd = lax.dot_general(v16, k_dec, TN_DIMS, preferred_element_type=F32)
    st_scr[hd] = st * jnp.exp2(b_last) + upd

    lane = lax.broadcasted_iota(jnp.int32, (SUBLANES, CHUNK), 1)
    srow = lax.broadcasted_iota(jnp.int32, (SUBLANES, CHUNK), 0)
    blocks = []
    for i in range(CHUNK // SUB):
        lo = i * SUB
        if i == 0:
            off = jnp.zeros((SUB, CHUNK), F32)
        else:
            b_ref = b[lo - 1:lo, :]
            qt = (q[lo:lo + SUB] * jnp.exp2(b[lo:lo + SUB] - b_ref)).astype(BF16)
            kt = jnp.concatenate([(k[0:lo] * jnp.exp2(b_ref - b[0:lo])).astype(BF16),
                                  jnp.zeros((CHUNK - lo, HG_D), BF16)], axis=0)
            off = lax.dot_general(qt, kt, NT_DIMS, preferred_element_type=F32)
        for half in range(SUB // SUBLANES):
            h0 = lo + half * SUBLANES
            qh = q[h0:h0 + SUBLANES]
            bh = b[h0:h0 + SUBLANES]
            sc = off[half * SUBLANES:(half + 1) * SUBLANES]
            for s in range((half + 1) * SUBLANES):
                ks = k_scr[r0 + lo + s:r0 + lo + s + 1, ln]
                bs = b_scr[r0 + lo + s:r0 + lo + s + 1, ln]
                col = jnp.sum(qh * ks * jnp.exp2(bh - bs), axis=-1, keepdims=True)
                keep = lane == lo + s
                if s >= half * SUBLANES:
                    keep = keep & (srow >= s - half * SUBLANES)
                sc = jnp.where(keep, col, sc)
            blocks.append(sc)
    scores = jnp.concatenate(blocks, axis=0).astype(BF16)
    return o + _dot(scores, v16)


def _layer_kernel(layer_ref, hm_ref, hx_ref, ng_ref, win_ref,
                  cw_ref, cb_ref, lng_ref, lnb_ref, wco_ref, lb_ref, hgn_ref, who_ref,
                  qn_ref, kn_ref, sinks_ref, wao_ref, wout_ref,
                  om_ref, ox_ref,
                  hn_scr, mix_scr, ext_scr, sh_scr, cwb_scr, ag_scr, za_scr, uh_scr, usw_scr, g_scr,
                  hq_scr, hk_scr, hb_scr, zb_scr, st_scr,
                  q_scr, kext_scr, vext_scr, oh_scr, zc_scr,
                  tri_scr, bdq_scr, bdk_scr, lbrow_scr):
    i = pl.program_id(0)
    layer = layer_ref[0]
    first_valid = jnp.where(i > 0, 0, FIRST_VALID)

    @pl.when(i == 0)
    def _init():
        ext_scr[0:CONV_PAD, :] = jnp.zeros((CONV_PAD, D_CONV), F32)
        st_scr[...] = jnp.zeros(st_scr.shape, F32)
        kext_scr[:, 0:KV_HEAD, :] = jnp.zeros((ATT_KV_HEADS, KV_HEAD, HEAD_DIM), BF16)
        vext_scr[:, 0:KV_HEAD, :] = jnp.zeros((ATT_KV_HEADS, KV_HEAD, HEAD_DIM), BF16)
        for j in range(CONV_WIDTH):
            cwb_scr[j] = jnp.broadcast_to(cw_ref[j:j + 1, :], (SUBLANES, D_CONV))
        r = lax.broadcasted_iota(jnp.int32, (CHUNK, CHUNK), 0)
        c = lax.broadcasted_iota(jnp.int32, (CHUNK, CHUNK), 1)
        tri_scr[...] = jnp.where(c <= r, 1.0, 0.0).astype(BF16)
        r = lax.broadcasted_iota(jnp.int32, (D_ATT, D_ATT), 0) // HEAD_DIM
        c = lax.broadcasted_iota(jnp.int32, (D_ATT, D_ATT), 1) // HEAD_DIM
        bdq_scr[...] = jnp.where(r == c, 1.0, 0.0).astype(BF16)
        r = lax.broadcasted_iota(jnp.int32, (D_KV, D_KV), 0) // HEAD_DIM
        c = lax.broadcasted_iota(jnp.int32, (D_KV, D_KV), 1) // HEAD_DIM
        bdk_scr[...] = jnp.where(r == c, 1.0, 0.0).astype(BF16)
        rows = [lb_ref[j:j + 1, :] for j in range(DEPTH)]
        mx = rows[0]
        for j in range(1, DEPTH):
            mx = jnp.maximum(mx, rows[j])
        ex = [jnp.exp(rw - mx) for rw in rows]
        tot = ex[0]
        for j in range(1, DEPTH):
            tot = tot + ex[j]
        sm = [e / tot for e in ex]
        cum = sm[0]
        lb_sel = jnp.zeros((1, D_HG), F32)
        for j in range(DEPTH):
            if j > 0:
                cum = cum + sm[j]
            lb_j = jnp.clip(cum - sm[0], 0.0, 1.0)
            lb_sel = jnp.where(layer == j, lb_j, lb_sel)
        lbrow_scr[...] = lb_sel

    def proj(dst_scr, w0, c0, n):
        def run():
            dst_scr[:, c0:c0 + n] = _dot(hn_scr[...], win_ref[:, w0 + c0:w0 + c0 + n])
        return run

    g_side = [proj(g_scr, OFF_G, c0, PIECE) for c0 in range(0, N_G, PIECE)]
    conv_side = ([proj(uh_scr, OFF_HG, c0, PIECE) for c0 in range(0, N_HG, PIECE)]
                 + [proj(usw_scr, OFF_SW, c0, min(PIECE, N_SW - c0)) for c0 in range(0, N_SW, PIECE)]
                 + g_side[:1])
    hgrn_side = g_side[1:]

    h = jnp.where(i == 0, hm_ref[...], hx_ref[...])
    ms = jnp.mean(h * h, axis=-1, keepdims=True)
    hn_scr[...] = (h * lax.rsqrt(ms + EPS) * ng_ref[...]).astype(BF16)

    row_t = lax.broadcasted_iota(jnp.int32, (TILE, 1), 0)
    for c0 in range(0, D_CONV, GLU_PIECE):
        u = (_dot(hn_scr[...], win_ref[:, c0:c0 + GLU_PIECE])
             * _sigmoid(_dot(hn_scr[...], win_ref[:, D_CONV + c0:D_CONV + c0 + GLU_PIECE])))
        ext_scr[CONV_PAD:CONV_PAD + TILE, c0:c0 + GLU_PIECE] = jnp.where(row_t >= first_valid, u, 0.0)
    ag_scr[...] = _dot(hn_scr[...], win_ref[:, 2 * D_CONV:N_A])
    for r in range(1, SUBLANES):
        sh_scr[r - 1] = ext_scr[r:r + CONV_SH_ROWS, :]
    for rb in range(TILE // CONV_RB):
        r0 = rb * CONV_RB
        acc = jnp.broadcast_to(cb_ref[...], (CONV_RB // SUBLANES, SUBLANES, D_CONV))
        for j in range(CONV_WIDTH):
            m = CONV_PAD - (CONV_WIDTH - 1) + j
            a0 = r0 + m - m % SUBLANES
            if m % SUBLANES == 0:
                src = ext_scr[a0:a0 + CONV_RB, :]
            else:
                src = sh_scr[m % SUBLANES - 1, a0:a0 + CONV_RB, :]
            acc = acc + cwb_scr[j][None] * src.reshape(CONV_RB // SUBLANES, SUBLANES, D_CONV)
        acc = acc.reshape(CONV_RB, D_CONV)
        mu = jnp.mean(acc, axis=-1, keepdims=True)
        d = acc - mu
        var = jnp.mean(d * d, axis=-1, keepdims=True)
        y = d * lax.rsqrt(var + EPS) * lng_ref[...] + lnb_ref[...]
        za_scr[r0:r0 + CONV_RB, :] = (_silu(y) * _silu(ag_scr[r0:r0 + CONV_RB, :])).astype(BF16)
        if rb < len(conv_side):
            conv_side[rb]()
    for run in conv_side[TILE // CONV_RB:]:
        run()
    ext_scr[0:CONV_PAD, :] = ext_scr[TILE:TILE + CONV_PAD, :]

    lb = lbrow_scr[...]
    tri = tri_scr[...]
    for c in range(NC):
        r0 = c * CHUNK
        rows = slice(r0, r0 + CHUNK)
        vrow = (lax.broadcasted_iota(jnp.int32, (CHUNK, 1), 0) + r0) >= first_valid
        zf = uh_scr[rows, D_HG:2 * D_HG]
        sg = jax.nn.sigmoid(zf)
        f = lb + (1.0 - lb) * sg
        logf = jnp.where(vrow, jnp.log(jnp.maximum(f, F_FLOOR)), 0.0)
        hk_scr[rows, :] = jnp.where(vrow, (1.0 - lb) * (1.0 - sg), 0.0)
        p0 = logf.astype(BF16)
        r1 = logf - p0.astype(F32)
        p1 = r1.astype(BF16)
        p2 = (r1 - p1.astype(F32)).astype(BF16)
        hb_scr[rows, :] = (_dot(tri, p0) + _dot(tri, p1) + _dot(tri, p2)) * LOG2E
        hq_scr[rows, :] = _silu(uh_scr[rows, 0:D_HG])
    side = list(hgrn_side)
    for c in range(NC):
        r0 = c * CHUNK
        rows = slice(r0, r0 + CHUNK)
        outs = []
        for hd in range(HG_HEADS):
            o = _hgrn_head(hq_scr, hk_scr, hb_scr, uh_scr, st_scr, r0, hd)
            o = o * lax.rsqrt(jnp.mean(o * o, axis=-1, keepdims=True) + EPS) * hgn_ref[...]
            outs.append(o)
            if side and hd % 2 == 1:
                side.pop(0)()
        o_all = jnp.concatenate(outs, axis=-1)
        zb_scr[rows, :] = (o_all * _silu(uh_scr[rows, 3 * D_HG:4 * D_HG])).astype(BF16)
    for run in side:
        run()
    mix_scr[...] = _gated(g_scr[:, 0:D_MODEL], _dot(za_scr[...], wco_ref[...]))

    q = usw_scr[:, 0:D_ATT]
    ssq = _dot((q * q).astype(BF16), bdq_scr[...])
    qn = (q * lax.rsqrt(ssq * (1.0 / HEAD_DIM) + EPS)
          * (qn_ref[...] * (HEAD_DIM ** -0.5 * LOG2E))).astype(BF16)
    for hq in range(ATT_Q_HEADS):
        q_scr[hq] = qn[:, hq * HEAD_DIM:(hq + 1) * HEAD_DIM]
    k = usw_scr[:, D_ATT:D_ATT + D_KV]
    ssk = _dot((k * k).astype(BF16), bdk_scr[...])
    kn = (k * lax.rsqrt(ssk * (1.0 / HEAD_DIM) + EPS) * kn_ref[...]).astype(BF16)
    vv = usw_scr[:, D_ATT + D_KV:D_ATT + 2 * D_KV].astype(BF16)
    for kv in range(ATT_KV_HEADS):
        kext_scr[kv, KV_HEAD:KV_HEAD + TILE, :] = kn[:, kv * HEAD_DIM:(kv + 1) * HEAD_DIM]
        vext_scr[kv, KV_HEAD:KV_HEAD + TILE, :] = vv[:, kv * HEAD_DIM:(kv + 1) * HEAD_DIM]

    grow = lax.broadcasted_iota(jnp.int32, (ATT_GROUP * CHUNK, 1), 0)
    sink_cols = []
    for kv in range(ATT_KV_HEADS):
        sc = jnp.zeros((ATT_GROUP * CHUNK, 1), F32)
        for j in range(ATT_GROUP):
            sc = jnp.where(grow // CHUNK == j, sinks_ref[layer, kv * ATT_GROUP + j] * LOG2E, sc)
        sink_cols.append(sc)
    col = lax.broadcasted_iota(jnp.int32, (1, BAND + CHUNK), 1)
    for c in range(NC):
        r0 = c * CHUNK
        cg = i * NC + c
        band_lo = FIRST_VALID - (cg - WINDOW_CHUNKS) * CHUNK
        meta_lo = jnp.where(cg > NC - 1 + WINDOW_CHUNKS, BAND + CHUNK - N_META, BAND + CHUNK)
        ok = ((col >= band_lo) & (col < BAND)) | (col >= meta_lo)
        bias = jnp.where(ok, 0.0, -jnp.inf)
        for kv in range(ATT_KV_HEADS):
            qg = jnp.concatenate([q_scr[kv * ATT_GROUP + j, r0:r0 + CHUNK, :]
                                  for j in range(ATT_GROUP)], axis=0)
            kb = jnp.concatenate([kext_scr[kv, r0 + CHUNK:r0 + CHUNK + BAND, :],
                                  kext_scr[kv, 0:CHUNK, :]], axis=0)
            vb = jnp.concatenate([vext_scr[kv, r0 + CHUNK:r0 + CHUNK + BAND, :],
                                  vext_scr[kv, 0:CHUNK, :]], axis=0)
            s = lax.dot_general(qg, kb, NT_DIMS, preferred_element_type=F32) + bias
            sink = sink_cols[kv]
            m = jnp.maximum(jnp.max(s, axis=-1, keepdims=True), sink)
            p = jnp.exp2(s - m)
            den = jnp.sum(p, axis=-1, keepdims=True) + jnp.exp2(sink - m)
            o = _dot(p.astype(BF16), vb) / den
            for j in range(ATT_GROUP):
                oh_scr[kv * ATT_GROUP + j, r0:r0 + CHUNK, :] = o[j * CHUNK:(j + 1) * CHUNK, :]
        if c == 0:
            mix_scr[...] += _gated(g_scr[:, D_MODEL:2 * D_MODEL], _dot(zb_scr[...], who_ref[...]))
    o_all = jnp.concatenate([oh_scr[hq] for hq in range(ATT_Q_HEADS)], axis=-1)
    zc_scr[...] = (o_all * _silu(usw_scr[:, D_ATT + 2 * D_KV:N_SW])).astype(BF16)

    for kv in range(ATT_KV_HEADS):
        for ext in (kext_scr, vext_scr):
            last = ext[kv, KV_HEAD + TILE - CHUNK:KV_HEAD + TILE, :]
            ext[kv, 0:CHUNK, :] = jnp.where(i == 0, last, ext[kv, 0:CHUNK, :])
            ext[kv, CHUNK:KV_HEAD, :] = ext[kv, TILE + CHUNK:TILE + KV_HEAD, :]

    mixed = 0.5 * (mix_scr[...] + _gated(g_scr[:, 2 * D_MODEL:3 * D_MODEL], _dot(zc_scr[...], wao_ref[...])))
    delta = _dot(mixed.astype(BF16), wout_ref[...])

    @pl.when(i == 0)
    def _store_meta():
        om_ref[...] = hm_ref[...] + delta
        ox_ref[...] = hx_ref[...]

    @pl.when(i > 0)
    def _store():
        ox_ref[...] = hx_ref[...] + delta


def _layer_call():
    def per_layer(shape):
        return pl.BlockSpec((None,) + shape, lambda i, l: (l[0],) + (0,) * len(shape),
                            pipeline_mode=pl.Buffered(1))

    def whole(shape):
        return pl.BlockSpec(shape, lambda i, l: (0,) * len(shape), pipeline_mode=pl.Buffered(1))

    x_spec = pl.BlockSpec((TILE, D_MODEL), lambda i, l: (jnp.maximum(i - 1, 0), 0))
    m_spec = pl.BlockSpec((TILE, D_MODEL), lambda i, l: (0, 0))
    in_specs = [
        m_spec,
        x_spec,
        per_layer((1, D_MODEL)),
        per_layer((D_MODEL, D_IN)),
        per_layer((CONV_WIDTH, D_CONV)),
        per_layer((1, D_CONV)),
        per_layer((1, D_CONV)),
        per_layer((1, D_CONV)),
        per_layer((D_CONV, D_MODEL)),
        whole((DEPTH, D_HG)),
        per_layer((1, HG_D)),
        per_layer((D_HG, D_MODEL)),
        per_layer((1, D_ATT)),
        per_layer((1, D_KV)),
        pl.BlockSpec(memory_space=pltpu.SMEM),
        per_layer((D_ATT, D_MODEL)),
        per_layer((D_MODEL, D_MODEL)),
    ]
    scratch = [
        pltpu.VMEM((TILE, D_MODEL), BF16),
        pltpu.VMEM((TILE, D_MODEL), F32),
        pltpu.VMEM((TILE + CONV_PAD, D_CONV), F32),
        pltpu.VMEM((SUBLANES - 1, CONV_SH_ROWS, D_CONV), F32),
        pltpu.VMEM((CONV_WIDTH, SUBLANES, D_CONV), F32),
        pltpu.VMEM((TILE, D_CONV), F32),
        pltpu.VMEM((TILE, D_CONV), BF16),
        pltpu.VMEM((TILE, N_HG), F32),
        pltpu.VMEM((TILE, N_SW), F32),
        pltpu.VMEM((TILE, N_G), F32),
        pltpu.VMEM((TILE, D_HG), F32),
        pltpu.VMEM((TILE, D_HG), F32),
        pltpu.VMEM((TILE, D_HG), F32),
        pltpu.VMEM((TILE, D_HG), BF16),
        pltpu.VMEM((HG_HEADS, HG_D, HG_D), F32),
        pltpu.VMEM((ATT_Q_HEADS, TILE, HEAD_DIM), BF16),
        pltpu.VMEM((ATT_KV_HEADS, KV_HEAD + TILE, HEAD_DIM), BF16),
        pltpu.VMEM((ATT_KV_HEADS, KV_HEAD + TILE, HEAD_DIM), BF16),
        pltpu.VMEM((ATT_Q_HEADS, TILE, HEAD_DIM), F32),
        pltpu.VMEM((TILE, D_ATT), BF16),
        pltpu.VMEM((CHUNK, CHUNK), BF16),
        pltpu.VMEM((D_ATT, D_ATT), BF16),
        pltpu.VMEM((D_KV, D_KV), BF16),
        pltpu.VMEM((1, D_HG), F32),
    ]
    return pl.pallas_call(
        _layer_kernel,
        grid_spec=pltpu.PrefetchScalarGridSpec(
            num_scalar_prefetch=1,
            grid=(N_TILES + 1,),
            in_specs=in_specs,
            out_specs=[m_spec, x_spec],
            scratch_shapes=scratch,
        ),
        out_shape=[jax.ShapeDtypeStruct((TILE, D_MODEL), F32),
                   jax.ShapeDtypeStruct((SEQ, D_MODEL), F32)],
        compiler_params=pltpu.CompilerParams(
            dimension_semantics=("arbitrary",),
            vmem_limit_bytes=VMEM_LIMIT_BYTES,
        ),
        name="trunk_layer",
    )


def kernel(x, meta_tokens, norm_g, w_in, conv_w, conv_b, conv_ln_g, conv_ln_b, w_conv_out,
           hg_lower_bounds, hg_norm_g, w_hg_out, q_norm_g, k_norm_g, attn_sinks, w_att_out, w_out):
    assert x.shape == (1, SEQ, D_MODEL) and w_in.shape == (DEPTH, D_MODEL, D_IN)
    hx = x[0].astype(F32)
    hm = jnp.concatenate([jnp.zeros((FIRST_VALID, D_MODEL), F32), meta_tokens.astype(F32)], axis=0)
    params = (
        norm_g.astype(F32)[:, None, :], w_in.astype(BF16),
        conv_w.astype(F32), conv_b.astype(F32)[:, None, :],
        conv_ln_g.astype(F32)[:, None, :], conv_ln_b.astype(F32)[:, None, :],
        w_conv_out.astype(BF16),
        hg_lower_bounds.astype(F32), hg_norm_g.astype(F32)[:, None, :], w_hg_out.astype(BF16),
        jnp.tile(q_norm_g.astype(F32), (1, ATT_Q_HEADS))[:, None, :],
        jnp.tile(k_norm_g.astype(F32), (1, ATT_KV_HEADS))[:, None, :],
        attn_sinks.astype(F32), w_att_out.astype(BF16), w_out.astype(BF16),
    )
    call = _layer_call()
    for layer in range(DEPTH):
        hm, hx = call(jnp.full((1,), layer, jnp.int32), hm, hx, *params)
    return hx[None].astype(x.dtype)
```

```python
import math

import jax
import jax.numpy as jnp
from jax import lax
from jax.experimental import pallas as pl
from jax.experimental.pallas import tpu as pltpu

F32 = jnp.float32
BF16 = jnp.bfloat16

D_MODEL = 1024
SEQ = 16384
DEPTH = 4
CHUNK = 64
N_META = 16
D_CONV = 512
CONV_WIDTH = 31
HG_HEADS = 4
HG_D = 128
D_HG = HG_HEADS * HG_D
F_FLOOR = 1e-30
ATT_Q_HEADS = 8
ATT_KV_HEADS = 2
HEAD_DIM = 64
ATT_GROUP = ATT_Q_HEADS // ATT_KV_HEADS
D_ATT = ATT_Q_HEADS * HEAD_DIM
D_KV = ATT_KV_HEADS * HEAD_DIM
WINDOW_CHUNKS = 2
EPS = 1e-6
LOG2E = math.log2(math.e)

N_A = 2 * D_CONV + D_CONV
N_HG = 4 * D_HG
N_SW = 2 * D_ATT + 2 * D_KV
N_G = 3 * D_MODEL
OFF_HG = N_A
OFF_SW = OFF_HG + N_HG
OFF_G = OFF_SW + N_SW
D_IN = OFF_G + N_G

TILE = 256
NC = TILE // CHUNK
N_TILES = SEQ // TILE
FIRST_VALID = TILE - N_META
SUBLANES = 8
CONV_PAD = 32
CONV_RB = 32
CONV_SH_ROWS = TILE + CONV_PAD - SUBLANES
SUB = 16
BAND = (WINDOW_CHUNKS + 1) * CHUNK
KV_HEAD = CHUNK + WINDOW_CHUNKS * CHUNK
PIECE = 512
GLU_PIECE = 256
W_CHUNK = 256
W_SLOTS = 2
VMEM_LIMIT_BYTES = 56 * 1024 * 1024

NT_DIMS = (((1,), (1,)), ((), ()))
TN_DIMS = (((0,), (0,)), ((), ()))


def _dot(a, b):
    return jnp.dot(a, b, preferred_element_type=F32)


def _sigmoid(x):
    return 0.5 * jnp.tanh(0.5 * x) + 0.5


def _silu(x):
    hx = 0.5 * x
    return hx * jnp.tanh(hx) + hx


def _gated(g, z):
    return z + jnp.tanh(0.5 * g) * z


def _hgrn_head(q_scr, k_scr, b_scr, v_ref, st_scr, r0, hd):
    ln = slice(hd * HG_D, (hd + 1) * HG_D)
    q = q_scr[r0:r0 + CHUNK, ln]
    k = k_scr[r0:r0 + CHUNK, ln]
    b = b_scr[r0:r0 + CHUNK, ln]
    v16 = v_ref[r0:r0 + CHUNK, 2 * D_HG + hd * HG_D:2 * D_HG + (hd + 1) * HG_D].astype(BF16)
    b_last = b[CHUNK - 1:CHUNK, :]
    st = st_scr[hd]
    q_dec = (q * jnp.exp2(b)).astype(BF16)
    o = lax.dot_general(q_dec, st.astype(BF16), NT_DIMS, preferred_element_type=F32)
    k_dec = (k * jnp.exp2(b_last - b)).astype(BF16)
    upd = lax.dot_general(v16, k_dec, TN_DIMS, preferred_element_type=F32)
    st_scr[hd] = st * jnp.exp2(b_last) + upd

    lane = lax.broadcasted_iota(jnp.int32, (SUBLANES, CHUNK), 1)
    srow = lax.broadcasted_iota(jnp.int32, (SUBLANES, CHUNK), 0)
    blocks = []
    for i in range(CHUNK // SUB):
        lo = i * SUB
        if i == 0:
            off = jnp.zeros((SUB, CHUNK), F32)
        else:
            b_ref = b[lo - 1:lo, :]
            qt = (q[lo:lo + SUB] * jnp.exp2(b[lo:lo + SUB] - b_ref)).astype(BF16)
            kt = jnp.concatenate([(k[0:lo] * jnp.exp2(b_ref - b[0:lo])).astype(BF16),
                                  jnp.zeros((CHUNK - lo, HG_D), BF16)], axis=0)
            off = lax.dot_general(qt, kt, NT_DIMS, preferred_element_type=F32)
        for half in range(SUB // SUBLANES):
            h0 = lo + half * SUBLANES
            qh = q[h0:h0 + SUBLANES]
            bh = b[h0:h0 + SUBLANES]
            sc = off[half * SUBLANES:(half + 1) * SUBLANES]
            for s in range((half + 1) * SUBLANES):
                ks = k_scr[r0 + lo + s:r0 + lo + s + 1, ln]
                bs = b_scr[r0 + lo + s:r0 + lo + s + 1, ln]
                col = jnp.sum(qh * ks * jnp.exp2(bh - bs), axis=-1, keepdims=True)
                keep = lane == lo + s
                if s >= half * SUBLANES:
                    keep = keep & (srow >= s - half * SUBLANES)
                sc = jnp.where(keep, col, sc)
            blocks.append(sc)
    scores = jnp.concatenate(blocks, axis=0).astype(BF16)
    return o + _dot(scores, v16)


def _weight_copy(w_hbm, stage_scr, sem, layer, c):
    slot = c % W_SLOTS
    return pltpu.make_async_copy(w_hbm.at[layer, :, pl.ds(c * W_CHUNK, W_CHUNK)],
                                 stage_scr.at[slot], sem.at[slot])


def _layer_kernel(layer_ref, hm_ref, hx_ref, ng_ref, w_hbm,
                  cw_ref, cb_ref, lng_ref, lnb_ref, wco_ref, lb_ref, hgn_ref, who_ref,
                  qn_ref, kn_ref, sinks_ref, wao_ref, wout_ref,
                  om_ref, ox_ref,
                  hn_scr, mix_scr, ext_scr, sh_scr, cwb_scr, ag_scr, za_scr, uh_scr, usw_scr, g_scr,
                  hq_scr, hk_scr, hb_scr, zb_scr, st_scr,
                  q_scr, kext_scr, vext_scr, oh_scr, zc_scr,
                  tri_scr, bdq_scr, bdk_scr, lbrow_scr, win_ref, wstage_scr, wsem):
    i = pl.program_id(0)
    layer = layer_ref[0]
    first_valid = jnp.where(i > 0, 0, FIRST_VALID)

    @pl.when(i == 0)
    def _init():
        _weight_copy(w_hbm, wstage_scr, wsem, layer, 0).start()
        for c in range(D_IN // W_CHUNK):
            if c + 1 < D_IN // W_CHUNK:
                _weight_copy(w_hbm, wstage_scr, wsem, layer, c + 1).start()
            _weight_copy(w_hbm, wstage_scr, wsem, layer, c).wait()
            win_ref[:, c * W_CHUNK:(c + 1) * W_CHUNK] = wstage_scr[c % W_SLOTS].astype(BF16)
        ext_scr[0:CONV_PAD, :] = jnp.zeros((CONV_PAD, D_CONV), F32)
        st_scr[...] = jnp.zeros(st_scr.shape, F32)
        kext_scr[:, 0:KV_HEAD, :] = jnp.zeros((ATT_KV_HEADS, KV_HEAD, HEAD_DIM), BF16)
        vext_scr[:, 0:KV_HEAD, :] = jnp.zeros((ATT_KV_HEADS, KV_HEAD, HEAD_DIM), BF16)
        for j in range(CONV_WIDTH):
            cwb_scr[j] = jnp.broadcast_to(cw_ref[j:j + 1, :], (SUBLANES, D_CONV))
        r = lax.broadcasted_iota(jnp.int32, (CHUNK, CHUNK), 0)
        c = lax.broadcasted_iota(jnp.int32, (CHUNK, CHUNK), 1)
        tri_scr[...] = jnp.where(c <= r, 1.0, 0.0).astype(BF16)
        r = lax.broadcasted_iota(jnp.int32, (D_ATT, D_ATT), 0) // HEAD_DIM
        c = lax.broadcasted_iota(jnp.int32, (D_ATT, D_ATT), 1) // HEAD_DIM
        bdq_scr[...] = jnp.where(r == c, 1.0, 0.0).astype(BF16)
        r = lax.broadcasted_iota(jnp.int32, (D_KV, D_KV), 0) // HEAD_DIM
        c = lax.broadcasted_iota(jnp.int32, (D_KV, D_KV), 1) // HEAD_DIM
        bdk_scr[...] = jnp.where(r == c, 1.0, 0.0).astype(BF16)
        rows = [lb_ref[j:j + 1, :] for j in range(DEPTH)]
        mx = rows[0]
        for j in range(1, DEPTH):
            mx = jnp.maximum(mx, rows[j])
        ex = [jnp.exp(rw - mx) for rw in rows]
        tot = ex[0]
        for j in range(1, DEPTH):
            tot = tot + ex[j]
        sm = [e / tot for e in ex]
        cum = sm[0]
        lb_sel = jnp.zeros((1, D_HG), F32)
        for j in range(DEPTH):
            if j > 0:
                cum = cum + sm[j]
            lb_j = jnp.clip(cum - sm[0], 0.0, 1.0)
            lb_sel = jnp.where(layer == j, lb_j, lb_sel)
        lbrow_scr[...] = lb_sel

    def proj(dst_scr, w0, c0, n):
        def run():
            dst_scr[:, c0:c0 + n] = _dot(hn_scr[...], win_ref[:, w0 + c0:w0 + c0 + n])
        return run

    g_side = [proj(g_scr, OFF_G, c0, PIECE) for c0 in range(0, N_G, PIECE)]
    conv_side = ([proj(uh_scr, OFF_HG, c0, PIECE) for c0 in range(0, N_HG, PIECE)]
                 + [proj(usw_scr, OFF_SW, c0, min(PIECE, N_SW - c0)) for c0 in range(0, N_SW, PIECE)]
                 + g_side[:1])
    hgrn_side = g_side[1:]

    h = jnp.where(i == 0, hm_ref[...], hx_ref[...])
    ms = jnp.mean(h * h, axis=-1, keepdims=True)
    hn_scr[...] = (h * lax.rsqrt(ms + EPS) * ng_ref[...]).astype(BF16)

    row_t = lax.broadcasted_iota(jnp.int32, (TILE, 1), 0)
    for c0 in range(0, D_CONV, GLU_PIECE):
        u = (_dot(hn_scr[...], win_ref[:, c0:c0 + GLU_PIECE])
             * _sigmoid(_dot(hn_scr[...], win_ref[:, D_CONV + c0:D_CONV + c0 + GLU_PIECE])))
        ext_scr[CONV_PAD:CONV_PAD + TILE, c0:c0 + GLU_PIECE] = jnp.where(row_t >= first_valid, u, 0.0)
    ag_scr[...] = _dot(hn_scr[...], win_ref[:, 2 * D_CONV:N_A])
    for r in range(1, SUBLANES):
        sh_scr[r - 1] = ext_scr[r:r + CONV_SH_ROWS, :]
    for rb in range(TILE // CONV_RB):
        r0 = rb * CONV_RB
        acc = jnp.broadcast_to(cb_ref[...], (CONV_RB // SUBLANES, SUBLANES, D_CONV))
        for j in range(CONV_WIDTH):
            m = CONV_PAD - (CONV_WIDTH - 1) + j
            a0 = r0 + m - m % SUBLANES
            if m % SUBLANES == 0:
                src = ext_scr[a0:a0 + CONV_RB, :]
            else:
                src = sh_scr[m % SUBLANES - 1, a0:a0 + CONV_RB, :]
            acc = acc + cwb_scr[j][None] * src.reshape(CONV_RB // SUBLANES, SUBLANES, D_CONV)
        acc = acc.reshape(CONV_RB, D_CONV)
        mu = jnp.mean(acc, axis=-1, keepdims=True)
        d = acc - mu
        var = jnp.mean(d * d, axis=-1, keepdims=True)
        y = d * lax.rsqrt(var + EPS) * lng_ref[...] + lnb_ref[...]
        za_scr[r0:r0 + CONV_RB, :] = (_silu(y) * _silu(ag_scr[r0:r0 + CONV_RB, :])).astype(BF16)
        if rb < len(conv_side):
            conv_side[rb]()
    for run in conv_side[TILE // CONV_RB:]:
        run()
    ext_scr[0:CONV_PAD, :] = ext_scr[TILE:TILE + CONV_PAD, :]

    lb = lbrow_scr[...]
    tri = tri_scr[...]
    for c in range(NC):
        r0 = c * CHUNK
        rows = slice(r0, r0 + CHUNK)
        vrow = (lax.broadcasted_iota(jnp.int32, (CHUNK, 1), 0) + r0) >= first_valid
        zf = uh_scr[rows, D_HG:2 * D_HG]
        sg = jax.nn.sigmoid(zf)
        f = lb + (1.0 - lb) * sg
        logf = jnp.where(vrow, jnp.log(jnp.maximum(f, F_FLOOR)), 0.0)
        hk_scr[rows, :] = jnp.where(vrow, (1.0 - lb) * (1.0 - sg), 0.0)
        p0 = logf.astype(BF16)
        r1 = logf - p0.astype(F32)
        p1 = r1.astype(BF16)
        p2 = (r1 - p1.astype(F32)).astype(BF16)
        hb_scr[rows, :] = (_dot(tri, p0) + _dot(tri, p1) + _dot(tri, p2)) * LOG2E
        hq_scr[rows, :] = _silu(uh_scr[rows, 0:D_HG])
    side = list(hgrn_side)
    for c in range(NC):
        r0 = c * CHUNK
        rows = slice(r0, r0 + CHUNK)
        outs = []
        for hd in range(HG_HEADS):
            o = _hgrn_head(hq_scr, hk_scr, hb_scr, uh_scr, st_scr, r0, hd)
            o = o * lax.rsqrt(jnp.mean(o * o, axis=-1, keepdims=True) + EPS) * hgn_ref[...]
            outs.append(o)
            if side and hd % 2 == 1:
                side.pop(0)()
        o_all = jnp.concatenate(outs, axis=-1)
        zb_scr[rows, :] = (o_all * _silu(uh_scr[rows, 3 * D_HG:4 * D_HG])).astype(BF16)
    for run in side:
        run()
    mix_scr[...] = _gated(g_scr[:, 0:D_MODEL], _dot(za_scr[...], wco_ref[...]))

    q = usw_scr[:, 0:D_ATT]
    ssq = _dot((q * q).astype(BF16), bdq_scr[...])
    qn = (q * lax.rsqrt(ssq * (1.0 / HEAD_DIM) + EPS)
          * (qn_ref[...] * (HEAD_DIM ** -0.5 * LOG2E))).astype(BF16)
    for hq in range(ATT_Q_HEADS):
        q_scr[hq] = qn[:, hq * HEAD_DIM:(hq + 1) * HEAD_DIM]
    k = usw_scr[:, D_ATT:D_ATT + D_KV]
    ssk = _dot((k * k).astype(BF16), bdk_scr[...])
    kn = (k * lax.rsqrt(ssk * (1.0 / HEAD_DIM) + EPS) * kn_ref[...]).astype(BF16)
    vv = usw_scr[:, D_ATT + D_KV:D_ATT + 2 * D_KV].astype(BF16)
    for kv in range(ATT_KV_HEADS):
        kext_scr[kv, KV_HEAD:KV_HEAD + TILE, :] = kn[:, kv * HEAD_DIM:(kv + 1) * HEAD_DIM]
        vext_scr[kv, KV_HEAD:KV_HEAD + TILE, :] = vv[:, kv * HEAD_DIM:(kv + 1) * HEAD_DIM]

    grow = lax.broadcasted_iota(jnp.int32, (ATT_GROUP * CHUNK, 1), 0)
    sink_cols = []
    for kv in range(ATT_KV_HEADS):
        sc = jnp.zeros((ATT_GROUP * CHUNK, 1), F32)
        for j in range(ATT_GROUP):
            sc = jnp.where(grow // CHUNK == j, sinks_ref[layer, kv * ATT_GROUP + j] * LOG2E, sc)
        sink_cols.append(sc)
    col = lax.broadcasted_iota(jnp.int32, (1, BAND + CHUNK), 1)
    for c in range(NC):
        r0 = c * CHUNK
        cg = i * NC + c
        band_lo = FIRST_VALID - (cg - WINDOW_CHUNKS) * CHUNK
        meta_lo = jnp.where(cg > NC - 1 + WINDOW_CHUNKS, BAND + CHUNK - N_META, BAND + CHUNK)
        ok = ((col >= band_lo) & (col < BAND)) | (col >= meta_lo)
        bias = jnp.where(ok, 0.0, -jnp.inf)
        for kv in range(ATT_KV_HEADS):
            qg = jnp.concatenate([q_scr[kv * ATT_GROUP + j, r0:r0 + CHUNK, :]
                                  for j in range(ATT_GROUP)], axis=0)
            kb = jnp.concatenate([kext_scr[kv, r0 + CHUNK:r0 + CHUNK + BAND, :],
                                  kext_scr[kv, 0:CHUNK, :]], axis=0)
            vb = jnp.concatenate([vext_scr[kv, r0 + CHUNK:r0 + CHUNK + BAND, :],
                                  vext_scr[kv, 0:CHUNK, :]], axis=0)
            s = lax.dot_general(qg, kb, NT_DIMS, preferred_element_type=F32) + bias
            sink = sink_cols[kv]
            m = jnp.maximum(jnp.max(s, axis=-1, keepdims=True), sink)
            p = jnp.exp2(s - m)
            den = jnp.sum(p, axis=-1, keepdims=True) + jnp.exp2(sink - m)
            o = _dot(p.astype(BF16), vb) / den
            for j in range(ATT_GROUP):
                oh_scr[kv * ATT_GROUP + j, r0:r0 + CHUNK, :] = o[j * CHUNK:(j + 1) * CHUNK, :]
        if c == 0:
            mix_scr[...] += _gated(g_scr[:, D_MODEL:2 * D_MODEL], _dot(zb_scr[...], who_ref[...]))
    o_all = jnp.concatenate([oh_scr[hq] for hq in range(ATT_Q_HEADS)], axis=-1)
    zc_scr[...] = (o_all * _silu(usw_scr[:, D_ATT + 2 * D_KV:N_SW])).astype(BF16)

    for kv in range(ATT_KV_HEADS):
        for ext in (kext_scr, vext_scr):
            last = ext[kv, KV_HEAD + TILE - CHUNK:KV_HEAD + TILE, :]
            ext[kv, 0:CHUNK, :] = jnp.where(i == 0, last, ext[kv, 0:CHUNK, :])
            ext[kv, CHUNK:KV_HEAD, :] = ext[kv, TILE + CHUNK:TILE + KV_HEAD, :]

    mixed = 0.5 * (mix_scr[...] + _gated(g_scr[:, 2 * D_MODEL:3 * D_MODEL], _dot(zc_scr[...], wao_ref[...])))
    delta = _dot(mixed.astype(BF16), wout_ref[...])

    @pl.when(i == 0)
    def _store_meta():
        om_ref[...] = hm_ref[...] + delta
        ox_ref[...] = hx_ref[...]

    @pl.when(i > 0)
    def _store():
        ox_ref[...] = hx_ref[...] + delta


def _layer_call():
    def per_layer(shape):
        return pl.BlockSpec((None,) + shape, lambda i, l: (l[0],) + (0,) * len(shape),
                            pipeline_mode=pl.Buffered(1))

    def whole(shape):
        return pl.BlockSpec(shape, lambda i, l: (0,) * len(shape), pipeline_mode=pl.Buffered(1))

    x_spec = pl.BlockSpec((TILE, D_MODEL), lambda i, l: (jnp.maximum(i - 1, 0), 0))
    m_spec = pl.BlockSpec((TILE, D_MODEL), lambda i, l: (0, 0))
    in_specs = [
        m_spec,
        x_spec,
        per_layer((1, D_MODEL)),
        pl.BlockSpec(memory_space=pl.ANY),
        per_layer((CONV_WIDTH, D_CONV)),
        per_layer((1, D_CONV)),
        per_layer((1, D_CONV)),
        per_layer((1, D_CONV)),
        per_layer((D_CONV, D_MODEL)),
        whole((DEPTH, D_HG)),
        per_layer((1, HG_D)),
        per_layer((D_HG, D_MODEL)),
        per_layer((1, D_ATT)),
        per_layer((1, D_KV)),
        pl.BlockSpec(memory_space=pltpu.SMEM),
        per_layer((D_ATT, D_MODEL)),
        per_layer((D_MODEL, D_MODEL)),
    ]
    scratch = [
        pltpu.VMEM((TILE, D_MODEL), BF16),
        pltpu.VMEM((TILE, D_MODEL), F32),
        pltpu.VMEM((TILE + CONV_PAD, D_CONV), F32),
        pltpu.VMEM((SUBLANES - 1, CONV_SH_ROWS, D_CONV), F32),
        pltpu.VMEM((CONV_WIDTH, SUBLANES, D_CONV), F32),
        pltpu.VMEM((TILE, D_CONV), F32),
        pltpu.VMEM((TILE, D_CONV), BF16),
        pltpu.VMEM((TILE, N_HG), F32),
        pltpu.VMEM((TILE, N_SW), F32),
        pltpu.VMEM((TILE, N_G), F32),
        pltpu.VMEM((TILE, D_HG), F32),
        pltpu.VMEM((TILE, D_HG), F32),
        pltpu.VMEM((TILE, D_HG), F32),
        pltpu.VMEM((TILE, D_HG), BF16),
        pltpu.VMEM((HG_HEADS, HG_D, HG_D), F32),
        pltpu.VMEM((ATT_Q_HEADS, TILE, HEAD_DIM), BF16),
        pltpu.VMEM((ATT_KV_HEADS, KV_HEAD + TILE, HEAD_DIM), BF16),
        pltpu.VMEM((ATT_KV_HEADS, KV_HEAD + TILE, HEAD_DIM), BF16),
        pltpu.VMEM((ATT_Q_HEADS, TILE, HEAD_DIM), F32),
        pltpu.VMEM((TILE, D_ATT), BF16),
        pltpu.VMEM((CHUNK, CHUNK), BF16),
        pltpu.VMEM((D_ATT, D_ATT), BF16),
        pltpu.VMEM((D_KV, D_KV), BF16),
        pltpu.VMEM((1, D_HG), F32),
        pltpu.VMEM((D_MODEL, D_IN), BF16),
        pltpu.VMEM((W_SLOTS, D_MODEL, W_CHUNK), F32),
        pltpu.SemaphoreType.DMA((W_SLOTS,)),
    ]
    return pl.pallas_call(
        _layer_kernel,
        grid_spec=pltpu.PrefetchScalarGridSpec(
            num_scalar_prefetch=1,
            grid=(N_TILES + 1,),
            in_specs=in_specs,
            out_specs=[m_spec, x_spec],
            scratch_shapes=scratch,
        ),
        out_shape=[jax.ShapeDtypeStruct((TILE, D_MODEL), F32),
                   jax.ShapeDtypeStruct((SEQ, D_MODEL), F32)],
        compiler_params=pltpu.CompilerParams(
            dimension_semantics=("arbitrary",),
            vmem_limit_bytes=VMEM_LIMIT_BYTES,
        ),
        name="trunk_layer",
    )


def kernel(x, meta_tokens, norm_g, w_in, conv_w, conv_b, conv_ln_g, conv_ln_b, w_conv_out,
           hg_lower_bounds, hg_norm_g, w_hg_out, q_norm_g, k_norm_g, attn_sinks, w_att_out, w_out):
    assert x.shape == (1, SEQ, D_MODEL) and w_in.shape == (DEPTH, D_MODEL, D_IN)
    hx = x[0].astype(F32)
    hm = jnp.concatenate([jnp.zeros((FIRST_VALID, D_MODEL), F32), meta_tokens.astype(F32)], axis=0)
    params = (
        norm_g.astype(F32)[:, None, :], w_in.astype(F32),
        conv_w.astype(F32), conv_b.astype(F32)[:, None, :],
        conv_ln_g.astype(F32)[:, None, :], conv_ln_b.astype(F32)[:, None, :],
        w_conv_out.astype(BF16),
        hg_lower_bounds.astype(F32), hg_norm_g.astype(F32)[:, None, :], w_hg_out.astype(BF16),
        jnp.tile(q_norm_g.astype(F32), (1, ATT_Q_HEADS))[:, None, :],
        jnp.tile(k_norm_g.astype(F32), (1, ATT_KV_HEADS))[:, None, :],
        attn_sinks.astype(F32), w_att_out.astype(BF16), w_out.astype(BF16),
    )
    call = _layer_call()
    for layer in range(DEPTH):
        hm, hx = call(jnp.full((1,), layer, jnp.int32), hm, hx, *params)
    return hx[None].astype(x.dtype)
```
